```python
import math
import jax, jax.numpy as jnp
from jax import lax
import numpy as np

D_MODEL = 2048
BATCH = 4
SEQ = 2048
DEPTH = 1
DEC_BATCH = 128
DEC_SEQ = 1
PAST_LEN = 16384
PAGE_SIZE = 128

D_LRU = D_MODEL
H_LRU = 8
LRU_BW = D_LRU // H_LRU
LRU_C = 8.0
CONV_W = 4
D_S5 = D_MODEL // 2
S5_H = 16
S5_G = D_S5 // S5_H
S5_P = 64
D_FF = ((8 * D_MODEL // 3 + 255) // 256) * 256
IN_WIDTH = D_LRU + D_S5 + 2 * D_MODEL
EPS = 1e-6

kernel_name = "hybrid_rglru_s5_gated_merge_decoder_step"


def rmsnorm(x, g):
    xf = x.astype(jnp.float32)
    y = xf * lax.rsqrt(jnp.mean(xf * xf, axis=-1, keepdims=True) + EPS)
    return (y * g.astype(jnp.float32)).astype(x.dtype)


def causal_dwconv(x, buf, w, b):
    L = x.shape[1]
    xx = jnp.concatenate([buf.astype(x.dtype), x], axis=1)
    y = b + xx[:, 0:L] * w[0]
    for k in range(1, CONV_W):
        y = y + xx[:, k:k + L] * w[k]
    return y.astype(x.dtype), xx[:, L:]


def rg_lru(x, h0, pos, wa, ba, wx, bx, lam):
    Bsz, L, _ = x.shape
    xf = x.astype(jnp.float32)
    xh = xf.reshape(Bsz, L, H_LRU, LRU_BW)
    r = jax.nn.sigmoid(jnp.einsum('blhi,hij->blhj', xh, wa.astype(jnp.float32)) + ba.astype(jnp.float32))
    i = jax.nn.sigmoid(jnp.einsum('blhi,hij->blhj', xh, wx.astype(jnp.float32)) + bx.astype(jnp.float32))
    r = r.reshape(Bsz, L, D_LRU)
    i = i.reshape(Bsz, L, D_LRU)
    log_a = -LRU_C * r * jax.nn.softplus(-lam.astype(jnp.float32))
    a = jnp.exp(log_a)
    mult = jnp.where((pos == 0)[None, :, None], 1.0, jnp.sqrt(-jnp.expm1(2.0 * log_a)))
    bt = mult * (i * xf)

    def step(h, ab):
        a_t, b_t = ab
        h = a_t * h + b_t
        return h, h

    hT, hs = lax.scan(step, h0.astype(jnp.float32), (jnp.swapaxes(a, 0, 1), jnp.swapaxes(bt, 0, 1)))
    return jnp.swapaxes(hs, 0, 1).astype(x.dtype), hT


def s5_layer(u, h0_re, h0_im, lam_re, lam_im, log_dt, b_re, b_im, c_re, c_im, d):
    Bsz, L, _ = u.shape
    f32 = jnp.float32
    uf = u.astype(f32).reshape(Bsz, L, S5_G, S5_H)
    lre, lim = lam_re.astype(f32), lam_im.astype(f32)
    dt = jnp.exp(log_dt.astype(f32))[:, None]
    mag = jnp.exp(lre * dt)
    ab_re = mag * jnp.cos(lim * dt)
    ab_im = mag * jnp.sin(lim * dt)
    e_re, e_im = ab_re - 1.0, ab_im
    den = lre * lre + lim * lim
    co_re = (e_re * lre + e_im * lim) / den
    co_im = (e_im * lre - e_re * lim) / den
    br, bi = b_re.astype(f32), b_im.astype(f32)
    bb_re = co_re[..., None] * br - co_im[..., None] * bi
    bb_im = co_re[..., None] * bi + co_im[..., None] * br
    bu_re = jnp.einsum('blgh,gph->blgp', uf, bb_re)
    bu_im = jnp.einsum('blgh,gph->blgp', uf, bb_im)
    a_re = jnp.broadcast_to(ab_re, (1, L, S5_G, S5_P))
    a_im = jnp.broadcast_to(ab_im, (1, L, S5_G, S5_P))

    def combine(e1, e2):
        a1r, a1i, b1r, b1i = e1
        a2r, a2i, b2r, b2i = e2
        return (a2r * a1r - a2i * a1i,
                a2r * a1i + a2i * a1r,
                a2r * b1r - a2i * b1i + b2r,
                a2r * b1i + a2i * b1r + b2i)

    pr, pim, sr, si = lax.associative_scan(combine, (a_re, a_im, bu_re, bu_im), axis=1)
    h0r = h0_re.astype(f32)[:, None]
    h0i = h0_im.astype(f32)[:, None]
    xr = pr * h0r - pim * h0i + sr
    xi = pr * h0i + pim * h0r + si
    y = (jnp.einsum('blgp,ghp->blgh', xr, c_re.astype(f32))
         - jnp.einsum('blgp,ghp->blgh', xi, c_im.astype(f32))
         + d.astype(f32) * uf)
    return y.reshape(Bsz, L, D_S5).astype(u.dtype), xr[:, -1], xi[:, -1]


def mixer_block(u, conv_buf, h0, s5_re0, s5_im0, pos, p):
    z = u @ p['w_in']
    x_lru = z[..., :D_LRU]
    x_s5 = z[..., D_LRU:D_LRU + D_S5]
    g_a = z[..., D_LRU + D_S5:D_LRU + D_S5 + D_MODEL]
    g_b = z[..., D_LRU + D_S5 + D_MODEL:]
    xc, new_buf = causal_dwconv(x_lru, conv_buf, p['lru_conv_w'], p['lru_conv_b'])
    ya, hT = rg_lru(xc, h0, pos, p['lru_wa'], p['lru_ba'], p['lru_wx'], p['lru_bx'], p['lru_lambda'])
    ya = ya @ p['lru_proj']
    ys, s_re, s_im = s5_layer(x_s5, s5_re0, s5_im0, p['s5_lambda_re'], p['s5_lambda_im'], p['s5_log_dt'],
                              p['s5_b_re'], p['s5_b_im'], p['s5_c_re'], p['s5_c_im'], p['s5_d'])
    v = jax.nn.gelu(ys)
    yb = (v @ p['s5_glu_wv']) * jax.nn.sigmoid(v @ p['s5_glu_wg'])
    merged = jax.nn.sigmoid(g_a) * ya + jax.nn.sigmoid(g_b) * yb
    return merged @ p['w_out'], new_buf, hT, s_re, s_im


def swiglu(u, wg, wu, wd):
    return (jax.nn.silu(u @ wg) * (u @ wu)) @ wd


def setup_inputs(seed: int = 0) -> dict:
    key = jax.random.key(seed)
    ks = jax.random.split(key, 32)
    nrm = jax.random.normal
    f32 = jnp.float32
    a0 = jax.random.uniform(ks[10], (DEPTH, D_LRU), f32, 0.9, 0.999)
    return {
        "x_prompt": nrm(ks[0], (BATCH, SEQ, D_MODEL), f32),
        "x_sample": nrm(ks[1], (DEC_BATCH, DEC_SEQ, D_MODEL), f32),
        "state_lru_conv": nrm(ks[2], (DEPTH, DEC_BATCH, CONV_W - 1, D_LRU), f32) * 0.5,
        "state_lru_h": nrm(ks[3], (DEPTH, DEC_BATCH, D_LRU), f32) * 0.5,
        "state_s5_re": nrm(ks[4], (DEPTH, DEC_BATCH, S5_G, S5_P), f32) * 0.5,
        "state_s5_im": nrm(ks[5], (DEPTH, DEC_BATCH, S5_G, S5_P), f32) * 0.5,
        "norm_mix_g": 1.0 + 0.01 * nrm(ks[6], (DEPTH, D_MODEL), f32),
        "w_in": nrm(ks[7], (DEPTH, D_MODEL, IN_WIDTH), f32) * D_MODEL ** -0.5,
        "lru_conv_w": nrm(ks[8], (DEPTH, CONV_W, D_LRU), f32) * CONV_W ** -0.5,
        "lru_conv_b": 0.01 * nrm(ks[9], (DEPTH, D_LRU), f32),
        "lru_wa": nrm(ks[11], (DEPTH, H_LRU, LRU_BW, LRU_BW), f32) * LRU_BW ** -0.5,
        "lru_ba": 0.01 * nrm(ks[12], (DEPTH, H_LRU, LRU_BW), f32),
        "lru_wx": nrm(ks[13], (DEPTH, H_LRU, LRU_BW, LRU_BW), f32) * LRU_BW ** -0.5,
        "lru_bx": 0.01 * nrm(ks[14], (DEPTH, H_LRU, LRU_BW), f32),
        "lru_lambda": jnp.log(a0) - jnp.log1p(-a0),
        "lru_proj": nrm(ks[15], (DEPTH, D_LRU, D_MODEL), f32) * D_LRU ** -0.5,
        "s5_lambda_re": -0.5 + 0.01 * nrm(ks[16], (DEPTH, S5_G, S5_P), f32),
        "s5_lambda_im": math.pi * jnp.arange(S5_P, dtype=f32)[None, None, :] + 0.01 * nrm(ks[17], (DEPTH, S5_G, S5_P), f32),
        "s5_log_dt": jax.random.uniform(ks[18], (DEPTH, S5_G), f32, math.log(0.001), math.log(0.1)),
        "s5_b_re": nrm(ks[19], (DEPTH, S5_G, S5_P, S5_H), f32) * (2.0 * S5_H) ** -0.5,
        "s5_b_im": nrm(ks[20], (DEPTH, S5_G, S5_P, S5_H), f32) * (2.0 * S5_H) ** -0.5,
        "s5_c_re": nrm(ks[21], (DEPTH, S5_G, S5_H, S5_P), f32) * (2.0 * S5_P) ** -0.5,
        "s5_c_im": nrm(ks[22], (DEPTH, S5_G, S5_H, S5_P), f32) * (2.0 * S5_P) ** -0.5,
        "s5_d": 0.5 * nrm(ks[23], (DEPTH, S5_G, S5_H), f32),
        "s5_glu_wv": nrm(ks[24], (DEPTH, D_S5, D_MODEL), f32) * D_S5 ** -0.5,
        "s5_glu_wg": nrm(ks[25], (DEPTH, D_S5, D_MODEL), f32) * D_S5 ** -0.5,
        "w_out": nrm(ks[26], (DEPTH, D_MODEL, D_MODEL), f32) * D_MODEL ** -0.5,
        "norm_ffn_g": 1.0 + 0.01 * nrm(ks[27], (DEPTH, D_MODEL), f32),
        "ffn_w_gate": nrm(ks[28], (DEPTH, D_MODEL, D_FF), f32) * D_MODEL ** -0.5,
        "ffn_w_up": nrm(ks[29], (DEPTH, D_MODEL, D_FF), f32) * D_MODEL ** -0.5,
        "ffn_w_down": nrm(ks[30], (DEPTH, D_FF, D_MODEL), f32) * D_FF ** -0.5,
        "norm_final_g": 1.0 + 0.01 * nrm(ks[31], (D_MODEL,), f32),
    }


def reference(x_prompt, x_sample, state_lru_conv, state_lru_h, state_s5_re, state_s5_im,
              norm_mix_g, w_in, lru_conv_w, lru_conv_b, lru_wa, lru_ba, lru_wx, lru_bx, lru_lambda,
              lru_proj, s5_lambda_re, s5_lambda_im, s5_log_dt, s5_b_re, s5_b_im, s5_c_re, s5_c_im,
              s5_d, s5_glu_wv, s5_glu_wg, w_out, norm_ffn_g, ffn_w_gate, ffn_w_up, ffn_w_down,
              norm_final_g):
    pos_p = jnp.arange(SEQ, dtype=jnp.int32)
    pos_s = PAST_LEN + jnp.arange(DEC_SEQ, dtype=jnp.int32)
    xp, xs = x_prompt, x_sample
    conv_p_l, h_p_l, sre_p_l, sim_p_l = [], [], [], []
    conv_s_l, h_s_l, sre_s_l, sim_s_l = [], [], [], []
    for l in range(DEPTH):
        p = dict(w_in=w_in[l], lru_conv_w=lru_conv_w[l], lru_conv_b=lru_conv_b[l],
                 lru_wa=lru_wa[l], lru_ba=lru_ba[l], lru_wx=lru_wx[l], lru_bx=lru_bx[l],
                 lru_lambda=lru_lambda[l], lru_proj=lru_proj[l],
                 s5_lambda_re=s5_lambda_re[l], s5_lambda_im=s5_lambda_im[l], s5_log_dt=s5_log_dt[l],
                 s5_b_re=s5_b_re[l], s5_b_im=s5_b_im[l], s5_c_re=s5_c_re[l], s5_c_im=s5_c_im[l],
                 s5_d=s5_d[l], s5_glu_wv=s5_glu_wv[l], s5_glu_wg=s5_glu_wg[l], w_out=w_out[l])
        up = rmsnorm(xp, norm_mix_g[l])
        yp, cb_p, h_p, sr_p, si_p = mixer_block(
            up, jnp.zeros((BATCH, CONV_W - 1, D_LRU), xp.dtype), jnp.zeros((BATCH, D_LRU), jnp.float32),
            jnp.zeros((BATCH, S5_G, S5_P), jnp.float32), jnp.zeros((BATCH, S5_G, S5_P), jnp.float32), pos_p, p)
        xp = xp + yp
        xp = xp + swiglu(rmsnorm(xp, norm_ffn_g[l]), ffn_w_gate[l], ffn_w_up[l], ffn_w_down[l])
        us = rmsnorm(xs, norm_mix_g[l])
        ys, cb_s, h_s, sr_s, si_s = mixer_block(
            us, state_lru_conv[l], state_lru_h[l], state_s5_re[l], state_s5_im[l], pos_s, p)
        xs = xs + ys
        xs = xs + swiglu(rmsnorm(xs, norm_ffn_g[l]), ffn_w_gate[l], ffn_w_up[l], ffn_w_down[l])
        conv_p_l.append(cb_p); h_p_l.append(h_p); sre_p_l.append(sr_p); sim_p_l.append(si_p)
        conv_s_l.append(cb_s); h_s_l.append(h_s); sre_s_l.append(sr_s); sim_s_l.append(si_s)
    y_prompt = rmsnorm(xp, norm_final_g)
    y_sample = rmsnorm(xs, norm_final_g)
    return (y_prompt, y_sample,
            jnp.stack(conv_p_l, 0), jnp.stack(h_p_l, 0), jnp.stack(sre_p_l, 0), jnp.stack(sim_p_l, 0),
            jnp.stack(conv_s_l, 0), jnp.stack(h_s_l, 0), jnp.stack(sre_s_l, 0), jnp.stack(sim_s_l, 0))
```

```python
import functools
import math

import jax
import jax.numpy as jnp
from jax import lax
from jax.experimental import pallas as pl
from jax.experimental.pallas import tpu as pltpu

F32 = jnp.float32
BF16 = jnp.bfloat16

D_MODEL = 2048
D_LRU = D_MODEL
H_LRU = 8
LRU_BW = D_LRU // H_LRU
LRU_C = 8.0
CONV_W = 4
D_S5 = D_MODEL // 2
S5_H = 16
S5_G = D_S5 // S5_H
S5_P = 64
S5_N = S5_G * S5_P
D_FF = 5632
IN_WIDTH = D_LRU + D_S5 + 2 * D_MODEL
EPS = 1e-6

S5_SLAB_G = 8
S5_SLABS = S5_G // S5_SLAB_G
S5_SLAB_IN = S5_SLAB_G * S5_H
S5_SLAB_ST = S5_SLAB_G * S5_P

LANES = 128
SUBLANES = 8
VMEM_LIMIT_BYTES = 56 * 1024 * 1024


def _params(sem):
    return pltpu.CompilerParams(dimension_semantics=sem, vmem_limit_bytes=VMEM_LIMIT_BYTES)


def _const_spec(shape):
    nd = len(shape)
    return pl.BlockSpec(shape, lambda *_: (0,) * nd, pipeline_mode=pl.Buffered(1))


def _rmsnorm(x, g):
    ms = jnp.mean(x * x, axis=-1, keepdims=True)
    return x * lax.rsqrt(ms + EPS) * g


def _s5prep_kernel(lre_ref, lim_ref, ldt_ref, br_ref, bi_ref, bbr_ref, bbi_ref, abr_ref, abi_ref):
    lre = lre_ref[...]
    lim = lim_ref[...]
    dt = jnp.exp(ldt_ref[...])
    mag = jnp.exp(lre * dt)
    ab_re = mag * jnp.cos(lim * dt)
    ab_im = mag * jnp.sin(lim * dt)
    e_re = ab_re - 1.0
    e_im = ab_im
    den = lre * lre + lim * lim
    co_re = (e_re * lre + e_im * lim) / den
    co_im = (e_im * lre - e_re * lim) / den
    br = br_ref[...]
    bi = bi_ref[...]
    bbr_ref[...] = co_re * br - co_im * bi
    bbi_ref[...] = co_re * bi + co_im * br
    abr_ref[...] = ab_re
    abi_ref[...] = ab_im


def _s5_prepare(lam_re, lam_im, log_dt, b_re, b_im, c_re, c_im):
    full = (S5_G, S5_P, S5_H)
    flat = (S5_G * S5_P * S5_H // LANES, LANES)
    bc = lambda a: jnp.broadcast_to(a, full).reshape(flat)
    args = (bc(lam_re[:, :, None]), bc(lam_im[:, :, None]), bc(log_dt[:, None, None]),
            b_re.reshape(flat), b_im.reshape(flat))
    sds = jax.ShapeDtypeStruct(flat, F32)
    bbr, bbi, abr, abi = pl.pallas_call(
        _s5prep_kernel, out_shape=(sds, sds, sds, sds), name="s5prep")(*args)
    ab_re = abr.reshape(full)[:, :, 0].reshape(1, S5_N)
    ab_im = abi.reshape(full)[:, :, 0].reshape(1, S5_N)
    eye = jnp.eye(S5_SLAB_G, dtype=F32)
    bb = jnp.stack([bbr.reshape(full), bbi.reshape(full)], 0)
    bb = bb.reshape(2, S5_SLABS, S5_SLAB_G, S5_P, S5_H)
    wb = jnp.einsum('cjgph,gk->jghckp', bb, eye).reshape(S5_SLABS, S5_SLAB_IN, 2 * S5_SLAB_ST)
    cc = jnp.stack([c_re, -c_im], 0).reshape(2, S5_SLABS, S5_SLAB_G, S5_H, S5_P)
    wc = jnp.einsum('cjghp,gk->jcgpkh', cc, eye).reshape(S5_SLABS, 2 * S5_SLAB_ST, S5_SLAB_IN)
    return ab_re, ab_im, wb.astype(BF16), wc.astype(BF16)


def _inproj_kernel(x_ref, g_ref, w_ref, z_ref, u_scr):
    @pl.when(pl.program_id(1) == 0)
    def _():
        u_scr[...] = _rmsnorm(x_ref[...], g_ref[...]).astype(BF16)

    z_ref[...] = jnp.dot(u_scr[...], w_ref[...], preferred_element_type=F32)


def _inproj(x, g, w, bm, bn):
    m, n = x.shape[0], w.shape[1]
    return pl.pallas_call(
        _inproj_kernel,
        grid=(m // bm, n // bn),
        in_specs=[pl.BlockSpec((bm, D_MODEL), lambda i, j: (i, 0)),
                  pl.BlockSpec((1, D_MODEL), lambda i, j: (0, 0)),
                  pl.BlockSpec((D_MODEL, bn), lambda i, j: (0, j))],
        out_specs=pl.BlockSpec((bm, bn), lambda i, j: (i, j)),
        out_shape=jax.ShapeDtypeStruct((m, n), F32),
        scratch_shapes=[pltpu.VMEM((bm, D_MODEL), BF16)],
        compiler_params=_params(("parallel", "arbitrary")),
        name="inproj",
    )(x, g, w)


def _seq_view(a, nb, seq_len):
    return a if seq_len == 1 else a.reshape(nb, seq_len, a.shape[-1])


def _seq_spec(nb, bt, width, col_block):
    if bt == 1:
        return pl.BlockSpec((nb, width), lambda c: (0, col_block))
    return pl.BlockSpec((nb, bt, width), lambda c: (0, c, col_block))


class _ScanLayout:
    def __init__(self, nb, bt, width):
        self.nb, self.bt, self.rows = nb, bt, nb * bt
        self.pack = max(1, SUBLANES // nb) if bt > 1 else 1
        self.n_cb = width // LANES
        self.n_slab = self.n_cb // self.pack
        self.shape = (self.n_slab, self.pack * self.rows, LANES)
        self.state_shape = (self.n_slab, self.pack * nb, LANES)

    def tile(self, cb):
        h = cb % self.pack
        return cb // self.pack, slice(h * self.rows, (h + 1) * self.rows), slice(None)

    def state(self, cb):
        h = cb % self.pack
        return cb // self.pack, slice(h * self.nb, (h + 1) * self.nb), slice(None)

    def step(self, t):
        if self.bt == 1:
            return slice(None)
        return pl.ds(t, self.pack * self.nb, stride=self.bt)

    def pack_state(self, dst, src_ref):
        for cb in range(self.n_cb):
            dst[self.state(cb)] = src_ref[:, cb * LANES:(cb + 1) * LANES]

    def unpack_state(self, dst_ref, src):
        for cb in range(self.n_cb):
            dst_ref[:, cb * LANES:(cb + 1) * LANES] = src[self.state(cb)]


def _lru_kernel(x_ref, p0_ref, p1_ref, p2_ref, h0_ref, cw_ref, cb_ref, wax_ref, bax_ref, lam_ref,
                proj_ref, ya_ref, n0_ref, n1_ref, n2_ref, ht_ref, xe_scr, a_scr, b_scr, h_scr,
                *, lay, first_pos_is_zero):
    c = pl.program_id(0)
    nb, bt, rows = lay.nb, lay.bt, lay.rows

    @pl.when(c == 0)
    def _():
        n0_ref[...] = p0_ref[...]
        n1_ref[...] = p1_ref[...]
        n2_ref[...] = p2_ref[...]
        lay.pack_state(h_scr, h0_ref)

    nlam = -lam_ref[...]
    sp = jnp.maximum(nlam, 0.0) + jnp.log1p(jnp.exp(-jnp.abs(nlam)))

    if bt > 1:
        xe_scr[:, 5, :] = n0_ref[...]
        xe_scr[:, 6, :] = n1_ref[...]
        xe_scr[:, 7, :] = n2_ref[...]
        xe_scr[:, 8:, :] = x_ref[...]
        is_first = (lax.broadcasted_iota(jnp.int32, (rows, LRU_BW), 0) % bt == 0) & (c == 0)

    for hb in range(H_LRU):
        cs = slice(hb * LRU_BW, (hb + 1) * LRU_BW)
        if bt > 1:
            xc = cb_ref[:, cs] + xe_scr[:, 5:5 + bt, cs] * cw_ref[0:1, cs]
            for k in range(1, CONV_W):
                xc = xc + xe_scr[:, 5 + k:5 + k + bt, cs] * cw_ref[k:k + 1, cs]
            xc = xc.reshape(rows, LRU_BW)
        else:
            xc = (cb_ref[:, cs] + n0_ref[:, cs] * cw_ref[0:1, cs] + n1_ref[:, cs] * cw_ref[1:2, cs]
                  + n2_ref[:, cs] * cw_ref[2:3, cs] + x_ref[:, cs] * cw_ref[3:4, cs])
        gates = jnp.dot(xc.astype(BF16), wax_ref[hb], preferred_element_type=F32) + bax_ref[hb]
        r = jax.nn.sigmoid(gates[:, :LRU_BW])
        i = jax.nn.sigmoid(gates[:, LRU_BW:])
        log_a = (-LRU_C) * r * sp[:, cs]
        a = jnp.exp(log_a)
        mult = jnp.sqrt(jnp.tanh(-log_a) * (1.0 + a * a))
        if first_pos_is_zero:
            mult = jnp.where(is_first, 1.0, mult)
        b = mult * (i * xc)
        for s in range(LRU_BW // LANES):
            cb = hb * (LRU_BW // LANES) + s
            a_scr[lay.tile(cb)] = a[:, s * LANES:(s + 1) * LANES]
            b_scr[lay.tile(cb)] = b[:, s * LANES:(s + 1) * LANES]

    if bt > 1:
        n0_ref[...] = x_ref[:, bt - 3, :]
        n1_ref[...] = x_ref[:, bt - 2, :]
        n2_ref[...] = x_ref[:, bt - 1, :]
    else:
        n0_ref[...] = n1_ref[...]
        n1_ref[...] = n2_ref[...]
        n2_ref[...] = x_ref[...]

    def step(t, hs):
        rs = lay.step(t)
        out = []
        for q in range(lay.n_slab):
            h = a_scr[q, rs, :] * hs[q] + b_scr[q, rs, :]
            b_scr[q, rs, :] = h
            out.append(h)
        return tuple(out)

    hs = tuple(h_scr[q] for q in range(lay.n_slab))
    hs = step(0, hs) if bt == 1 else lax.fori_loop(0, bt, step, hs)
    for q in range(lay.n_slab):
        h_scr[q] = hs[q]
    lay.unpack_state(ht_ref, h_scr)

    hseq = jnp.concatenate([b_scr[lay.tile(cb)].astype(BF16) for cb in range(lay.n_cb)], axis=1)
    ya = jnp.dot(hseq, proj_ref[...], preferred_element_type=F32)
    ya_ref[...] = ya.reshape(ya_ref.shape)


def _lru(z, conv0, h0, cw, cb, wax, bax, lam, proj, *, nb, seq_len, bt, first_pos_is_zero):
    lay = _ScanLayout(nb, bt, D_LRU)
    p0, p1, p2 = conv0[:, 0, :], conv0[:, 1, :], conv0[:, 2, :]
    st = pl.BlockSpec((nb, D_LRU), lambda c: (0, 0))
    st_sds = jax.ShapeDtypeStruct((nb, D_LRU), F32)
    ya_sds = jax.ShapeDtypeStruct((nb, D_LRU) if bt == 1 else (nb, seq_len, D_LRU), F32)
    kern = functools.partial(_lru_kernel, lay=lay, first_pos_is_zero=first_pos_is_zero)
    xe_shape = (nb, bt + SUBLANES, D_LRU) if bt > 1 else (SUBLANES, LANES)
    ya, n0, n1, n2, ht = pl.pallas_call(
        kern,
        grid=(seq_len // bt,),
        in_specs=[_seq_spec(nb, bt, D_LRU, 0), st, st, st, st,
                  _const_spec((CONV_W, D_LRU)), _const_spec((1, D_LRU)),
                  _const_spec((H_LRU, LRU_BW, 2 * LRU_BW)), _const_spec((H_LRU, 1, 2 * LRU_BW)),
                  _const_spec((1, D_LRU)), _const_spec((D_LRU, D_MODEL))],
        out_specs=(_seq_spec(nb, bt, D_LRU, 0), st, st, st, st),
        out_shape=(ya_sds, st_sds, st_sds, st_sds, st_sds),
        scratch_shapes=[pltpu.VMEM(xe_shape, F32), pltpu.VMEM(lay.shape, F32),
                        pltpu.VMEM(lay.shape, F32), pltpu.VMEM(lay.state_shape, F32)],
        compiler_params=_params(("arbitrary",)),
        name="lru",
    )(_seq_view(z, nb, seq_len), p0, p1, p2, h0, cw, cb, wax, bax, lam, proj)
    return ya.reshape(nb * seq_len, D_LRU), jnp.stack([n0, n1, n2], axis=1), ht


def _gelu_tanh(x):
    c = math.sqrt(2.0 / math.pi)
    return 0.5 * x * (1.0 + jnp.tanh(c * (x + 0.044715 * (x * x * x))))


def _s5_kernel(u_ref, s0r_ref, s0i_ref, abr_ref, abi_ref, wb_ref, wc_ref, d_ref, wv_ref, wg_ref,
               yb_ref, sr_ref, si_ref, xr_scr, xi_scr, pr_scr, pi_scr, y_scr, *, lay):
    c = pl.program_id(0)
    bt, rows = lay.bt, lay.rows
    cb_per_slab = S5_SLAB_ST // LANES

    @pl.when(c == 0)
    def _():
        lay.pack_state(pr_scr, s0r_ref)
        lay.pack_state(pi_scr, s0i_ref)

    u = u_ref[...].reshape(rows, D_S5)
    ub = u.astype(BF16)
    for j in range(S5_SLABS):
        bu = jnp.dot(ub[:, j * S5_SLAB_IN:(j + 1) * S5_SLAB_IN], wb_ref[j],
                     preferred_element_type=F32)
        for s in range(cb_per_slab):
            cb = j * cb_per_slab + s
            xr_scr[lay.tile(cb)] = bu[:, s * LANES:(s + 1) * LANES]
            xi_scr[lay.tile(cb)] = bu[:, S5_SLAB_ST + s * LANES:S5_SLAB_ST + (s + 1) * LANES]

    def advance(rs, prev_r, prev_i):
        for q in range(lay.n_slab):
            ar, ai = abr_ref[q], abi_ref[q]
            pr, pi = prev_r(q), prev_i(q)
            nr = ar * pr - ai * pi + xr_scr[q, rs, :]
            ni = ar * pi + ai * pr + xi_scr[q, rs, :]
            xr_scr[q, rs, :] = nr
            xi_scr[q, rs, :] = ni

    advance(lay.step(0), lambda q: pr_scr[q], lambda q: pi_scr[q])
    if bt > 1:
        def step(t, carry):
            prev = lay.step(t - 1)
            advance(lay.step(t), lambda q: xr_scr[q, prev, :], lambda q: xi_scr[q, prev, :])
            return carry

        lax.fori_loop(1, bt, step, 0)
    last = lay.step(bt - 1)
    for q in range(lay.n_slab):
        pr_scr[q] = xr_scr[q, last, :]
        pi_scr[q] = xi_scr[q, last, :]
    lay.unpack_state(sr_ref, pr_scr)
    lay.unpack_state(si_ref, pi_scr)

    for j in range(S5_SLABS):
        cbs = range(j * cb_per_slab, (j + 1) * cb_per_slab)
        xs = jnp.concatenate([xr_scr[lay.tile(cb)].astype(BF16) for cb in cbs]
                             + [xi_scr[lay.tile(cb)].astype(BF16) for cb in cbs], axis=1)
        y_scr[:, j * S5_SLAB_IN:(j + 1) * S5_SLAB_IN] = jnp.dot(
            xs, wc_ref[j], preferred_element_type=F32)
    y = y_scr[...] + d_ref[...] * u
    v = _gelu_tanh(y).astype(BF16)
    yv = jnp.dot(v, wv_ref[...], preferred_element_type=F32)
    yg = jnp.dot(v, wg_ref[...], preferred_element_type=F32)
    yb_ref[...] = (yv * jax.nn.sigmoid(yg)).reshape(yb_ref.shape)


def _pack_coeff(ab, lay):
    reps = lay.nb if lay.bt > 1 else 1
    a = ab.reshape(lay.n_slab, lay.pack, 1, LANES)
    a = jnp.broadcast_to(a, (lay.n_slab, lay.pack, reps, LANES))
    return a.reshape(lay.n_slab, lay.pack * reps, LANES)


def _s5(z, s0r, s0i, ab_re, ab_im, wb, wc, d, wv, wg, *, nb, seq_len, bt):
    lay = _ScanLayout(nb, bt, S5_N)
    rows = lay.rows
    st = pl.BlockSpec((nb, S5_N), lambda c: (0, 0))
    st_sds = jax.ShapeDtypeStruct((nb, S5_N), F32)
    yb_sds = jax.ShapeDtypeStruct((nb, D_MODEL) if bt == 1 else (nb, seq_len, D_MODEL), F32)
    x_s5_block = (D_LRU + 2 * D_MODEL) // D_S5
    abr, abi = _pack_coeff(ab_re, lay), _pack_coeff(ab_im, lay)
    yb, sr, si = pl.pallas_call(
        functools.partial(_s5_kernel, lay=lay),
        grid=(seq_len // bt,),
        in_specs=[_seq_spec(nb, bt, D_S5, x_s5_block), st, st,
                  _const_spec(abr.shape), _const_spec(abi.shape),
                  _const_spec((S5_SLABS, S5_SLAB_IN, 2 * S5_SLAB_ST)),
                  _const_spec((S5_SLABS, 2 * S5_SLAB_ST, S5_SLAB_IN)),
                  _const_spec((1, D_S5)), _const_spec((D_S5, D_MODEL)), _const_spec((D_S5, D_MODEL))],
        out_specs=(_seq_spec(nb, bt, D_MODEL, 0), st, st),
        out_shape=(yb_sds, st_sds, st_sds),
        scratch_shapes=[pltpu.VMEM(lay.shape, F32), pltpu.VMEM(lay.shape, F32),
                        pltpu.VMEM(lay.state_shape, F32), pltpu.VMEM(lay.state_shape, F32),
                        pltpu.VMEM((rows, D_S5), F32)],
        compiler_params=_params(("arbitrary",)),
        name="s5",
    )(_seq_view(z, nb, seq_len), s0r, s0i, abr, abi, wb, wc, d, wv, wg)
    return yb.reshape(nb * seq_len, D_MODEL), sr, si


def _merge_kernel(ya_ref, yb_ref, ga_ref, gb_ref, x_ref, w_ref, g_ref, x1_ref, u2_ref):
    merged = jax.nn.sigmoid(ga_ref[...]) * ya_ref[...] + jax.nn.sigmoid(gb_ref[...]) * yb_ref[...]
    x1 = x_ref[...] + jnp.dot(merged.astype(BF16), w_ref[...], preferred_element_type=F32)
    x1_ref[...] = x1
    u2_ref[...] = _rmsnorm(x1, g_ref[...]).astype(BF16)


def _merge(ya, yb, z, x, w_out, g, bm):
    m = x.shape[0]
    row = lambda col: pl.BlockSpec((bm, D_MODEL), lambda i: (i, col))
    return pl.pallas_call(
        _merge_kernel,
        grid=(m // bm,),
        in_specs=[row(0), row(0), row(1), row(2), row(0),
                  _const_spec((D_MODEL, D_MODEL)), _const_spec((1, D_MODEL))],
        out_specs=(row(0), row(0)),
        out_shape=(jax.ShapeDtypeStruct((m, D_MODEL), F32), jax.ShapeDtypeStruct((m, D_MODEL), BF16)),
        compiler_params=_params(("parallel",)),
        name="merge",
    )(ya, yb, z, z, x, w_out, g)


def _ffn_kernel(u_ref, x1_ref, wg_ref, wu_ref, wd_ref, g_ref, y_ref, acc_scr):
    f = pl.program_id(1)

    @pl.when(f == 0)
    def _():
        acc_scr[...] = x1_ref[...]

    u = u_ref[...]
    hg = jnp.dot(u, wg_ref[...], preferred_element_type=F32)
    hu = jnp.dot(u, wu_ref[...], preferred_element_type=F32)
    hid = (hg * jax.nn.sigmoid(hg) * hu).astype(BF16)
    acc_scr[...] += jnp.dot(hid, wd_ref[...], preferred_element_type=F32)

    @pl.when(f == pl.num_programs(1) - 1)
    def _():
        y_ref[...] = _rmsnorm(acc_scr[...], g_ref[...])


def _ffn(u2, x1, wg, wu, wd, g, bm, bf):
    m = x1.shape[0]
    return pl.pallas_call(
        _ffn_kernel,
        grid=(m // bm, D_FF // bf),
        in_specs=[pl.BlockSpec((bm, D_MODEL), lambda i, f: (i, 0)),
                  pl.BlockSpec((bm, D_MODEL), lambda i, f: (i, 0)),
                  pl.BlockSpec((D_MODEL, bf), lambda i, f: (0, f)),
                  pl.BlockSpec((D_MODEL, bf), lambda i, f: (0, f)),
                  pl.BlockSpec((bf, D_MODEL), lambda i, f: (f, 0)),
                  pl.BlockSpec((1, D_MODEL), lambda i, f: (0, 0))],
        out_specs=pl.BlockSpec((bm, D_MODEL), lambda i, f: (i, 0)),
        out_shape=jax.ShapeDtypeStruct((m, D_MODEL), F32),
        scratch_shapes=[pltpu.VMEM((bm, D_MODEL), F32)],
        compiler_params=_params(("parallel", "arbitrary")),
        name="ffn",
    )(u2, x1, wg, wu, wd, g)


def _layer(x, conv0, h0, s0r, s0i, w, *, nb, seq_len, bt_lru, bt_s5, bm, bm_merge, bn, bf,
           first_pos_is_zero):
    z = _inproj(x, w['norm_mix_g'], w['w_in'], bm, bn)
    ya, conv_n, h_n = _lru(z, conv0, h0, w['cw'], w['cb'], w['wax'], w['bax'], w['lam'], w['proj'],
                           nb=nb, seq_len=seq_len, bt=bt_lru, first_pos_is_zero=first_pos_is_zero)
    yb, sr, si = _s5(z, s0r, s0i, w['ab_re'], w['ab_im'], w['wb'], w['wc'], w['d'], w['wv'], w['wg'],
                     nb=nb, seq_len=seq_len, bt=bt_s5)
    x1, u2 = _merge(ya, yb, z, x, w['w_out'], w['norm_ffn_g'], bm_merge)
    y = _ffn(u2, x1, w['ffn_wg'], w['ffn_wu'], w['ffn_wd'], w['norm_final_g'], bm, bf)
    return y, conv_n, h_n, sr, si


def kernel(x_prompt, x_sample, state_lru_conv, state_lru_h, state_s5_re, state_s5_im, norm_mix_g, w_in, lru_conv_w, lru_conv_b, lru_wa, lru_ba, lru_wx, lru_bx, lru_lambda, lru_proj, s5_lambda_re, s5_lambda_im, s5_log_dt, s5_b_re, s5_b_im, s5_c_re, s5_c_im, s5_d, s5_glu_wv, s5_glu_wg, w_out, norm_ffn_g, ffn_w_gate, ffn_w_up, ffn_w_down, norm_final_g):
    batch, seq, _ = x_prompt.shape
    dec_batch = x_sample.shape[0]
    l = 0
    ab_re, ab_im, wb, wc = _s5_prepare(s5_lambda_re[l], s5_lambda_im[l], s5_log_dt[l],
                                       s5_b_re[l], s5_b_im[l], s5_c_re[l], s5_c_im[l])
    wi = w_in[l]
    wi = jnp.concatenate([wi[:, :D_LRU], wi[:, D_LRU + D_S5:], wi[:, D_LRU:D_LRU + D_S5]], axis=1)
    w = dict(
        norm_mix_g=norm_mix_g[l][None, :], w_in=wi.astype(BF16),
        cw=lru_conv_w[l], cb=lru_conv_b[l][None, :],
        wax=jnp.concatenate([lru_wa[l], lru_wx[l]], axis=-1).astype(BF16),
        bax=jnp.concatenate([lru_ba[l], lru_bx[l]], axis=-1)[:, None, :],
        lam=lru_lambda[l][None, :], proj=lru_proj[l].astype(BF16),
        ab_re=ab_re, ab_im=ab_im, wb=wb, wc=wc, d=s5_d[l].reshape(1, D_S5),
        wv=s5_glu_wv[l].astype(BF16), wg=s5_glu_wg[l].astype(BF16),
        w_out=w_out[l].astype(BF16), norm_ffn_g=norm_ffn_g[l][None, :],
        ffn_wg=ffn_w_gate[l].astype(BF16), ffn_wu=ffn_w_up[l].astype(BF16),
        ffn_wd=ffn_w_down[l].astype(BF16), norm_final_g=norm_final_g[None, :],
    )
    zeros = lambda *s: jnp.zeros(s, F32)
    yp, conv_p, h_p, sr_p, si_p = _layer(
        x_prompt.reshape(batch * seq, D_MODEL), zeros(batch, CONV_W - 1, D_LRU), zeros(batch, D_LRU),
        zeros(batch, S5_N), zeros(batch, S5_N), w,
        nb=batch, seq_len=seq, bt_lru=128, bt_s5=64, bm=512, bm_merge=256, bn=1024, bf=512,
        first_pos_is_zero=True)
    ys, conv_s, h_s, sr_s, si_s = _layer(
        x_sample.reshape(dec_batch, D_MODEL), state_lru_conv[l], state_lru_h[l],
        state_s5_re[l].reshape(dec_batch, S5_N), state_s5_im[l].reshape(dec_batch, S5_N), w,
        nb=dec_batch, seq_len=1, bt_lru=1, bt_s5=1, bm=dec_batch, bm_merge=dec_batch, bn=1024, bf=512,
        first_pos_is_zero=False)
    st = lambda a, n: a.reshape(1, n, S5_G, S5_P)
    return (yp.reshape(batch, seq, D_MODEL), ys.reshape(dec_batch, 1, D_MODEL),
            conv_p[None], h_p[None], st(sr_p, batch), st(si_p, batch),
            conv_s[None], h_s[None], st(sr_s, dec_batch), st(si_s, dec_batch))
```

```python
import functools
import math

import jax
import jax.numpy as jnp
from jax import lax
from jax.experimental import pallas as pl
from jax.experimental.pallas import tpu as pltpu

F32 = jnp.float32
BF16 = jnp.bfloat16

D_MODEL = 2048
D_LRU = D_MODEL
H_LRU = 8
LRU_BW = D_LRU // H_LRU
LRU_C = 8.0
CONV_W = 4
D_S5 = D_MODEL // 2
S5_H = 16
S5_G = D_S5 // S5_H
S5_P = 64
S5_N = S5_G * S5_P
D_FF = 5632
IN_WIDTH = D_LRU + D_S5 + 2 * D_MODEL
EPS = 1e-6

S5_SLAB_G = 8
S5_SLABS = S5_G // S5_SLAB_G
S5_SLAB_IN = S5_SLAB_G * S5_H
S5_SLAB_ST = S5_SLAB_G * S5_P

LANES = 128
SUBLANES = 8
HALF = SUBLANES // 2
VMEM_LIMIT_BYTES = 56 * 1024 * 1024


def _params(sem):
    return pltpu.CompilerParams(dimension_semantics=sem, vmem_limit_bytes=VMEM_LIMIT_BYTES)


def _const_spec(shape):
    nd = len(shape)
    return pl.BlockSpec(shape, lambda *_: (0,) * nd, pipeline_mode=pl.Buffered(1))


def _rmsnorm(x, g):
    ms = jnp.mean(x * x, axis=-1, keepdims=True)
    return x * lax.rsqrt(ms + EPS) * g


def _lo_rows(shape):
    return lax.broadcasted_iota(jnp.int32, shape, 0) % SUBLANES < HALF


def _s5prep_kernel(lre_ref, lim_ref, ldt_ref, br_ref, bi_ref,
                   bbr_ref, bbi_ref, abbr_ref, abbi_ref, abr_ref, abi_ref, a2r_ref, a2i_ref):
    lre = lre_ref[...]
    lim = lim_ref[...]
    dt = jnp.exp(ldt_ref[...])
    mag = jnp.exp(lre * dt)
    ab_re = mag * jnp.cos(lim * dt)
    ab_im = mag * jnp.sin(lim * dt)
    e_re = ab_re - 1.0
    e_im = ab_im
    den = lre * lre + lim * lim
    co_re = (e_re * lre + e_im * lim) / den
    co_im = (e_im * lre - e_re * lim) / den
    br = br_ref[...]
    bi = bi_ref[...]
    bb_re = co_re * br - co_im * bi
    bb_im = co_re * bi + co_im * br
    bbr_ref[...] = bb_re
    bbi_ref[...] = bb_im
    abbr_ref[...] = ab_re * bb_re - ab_im * bb_im
    abbi_ref[...] = ab_re * bb_im + ab_im * bb_re
    abr_ref[...] = ab_re
    abi_ref[...] = ab_im
    a2r_ref[...] = ab_re * ab_re - ab_im * ab_im
    a2i_ref[...] = 2.0 * (ab_re * ab_im)


def _s5_prepare(lam_re, lam_im, log_dt, b_re, b_im, c_re, c_im):
    full = (S5_G, S5_P, S5_H)
    flat = (S5_G * S5_P * S5_H // LANES, LANES)
    bc = lambda a: jnp.broadcast_to(a, full).reshape(flat)
    args = (bc(lam_re[:, :, None]), bc(lam_im[:, :, None]), bc(log_dt[:, None, None]),
            b_re.reshape(flat), b_im.reshape(flat))
    sds = jax.ShapeDtypeStruct(flat, F32)
    bbr, bbi, abbr, abbi, abr, abi, a2r, a2i = pl.pallas_call(
        _s5prep_kernel, out_shape=(sds,) * 8, name="s5prep")(*args)
    per_state = lambda a: a.reshape(full)[:, :, 0].reshape(1, S5_N)
    coeff = lambda a1, a2: jnp.concatenate(
        [jnp.broadcast_to(per_state(a1), (HALF, S5_N)), jnp.broadcast_to(per_state(a2), (HALF, S5_N))], 0)
    ca_re, ca_im = coeff(abr, a2r), coeff(abi, a2i)
    eye = jnp.eye(S5_SLAB_G, dtype=F32)

    def slab_b(re, im):
        b = jnp.stack([re.reshape(full), im.reshape(full)], 0)
        b = b.reshape(2, S5_SLABS, S5_SLAB_G, S5_P, S5_H)
        return jnp.einsum('cjgph,gk->jghckp', b, eye).reshape(S5_SLABS, S5_SLAB_IN, 2 * S5_SLAB_ST)

    wb = jnp.concatenate([slab_b(bbr, bbi), slab_b(abbr, abbi)], axis=1)
    cc = jnp.stack([c_re, -c_im], 0).reshape(2, S5_SLABS, S5_SLAB_G, S5_H, S5_P)
    wc = jnp.einsum('cjghp,gk->jcgpkh', cc, eye).reshape(S5_SLABS, 2 * S5_SLAB_ST, S5_SLAB_IN)
    return ca_re, ca_im, wb.astype(BF16), wc.astype(BF16)


def _inproj_kernel(x_ref, g_ref, w_ref, z_ref, u_scr):
    @pl.when(pl.program_id(1) == 0)
    def _():
        u_scr[...] = _rmsnorm(x_ref[...], g_ref[...]).astype(BF16)

    z_ref[...] = jnp.dot(u_scr[...], w_ref[...], preferred_element_type=F32)


def _inproj(x, g, w, bm, bn):
    m, n = x.shape[0], w.shape[1]
    return pl.pallas_call(
        _inproj_kernel,
        grid=(m // bm, n // bn),
        in_specs=[pl.BlockSpec((bm, D_MODEL), lambda i, j: (i, 0)),
                  pl.BlockSpec((1, D_MODEL), lambda i, j: (0, 0)),
                  pl.BlockSpec((D_MODEL, bn), lambda i, j: (0, j))],
        out_specs=pl.BlockSpec((bm, bn), lambda i, j: (i, j)),
        out_shape=jax.ShapeDtypeStruct((m, n), F32),
        scratch_shapes=[pltpu.VMEM((bm, D_MODEL), BF16)],
        compiler_params=_params(("parallel", "arbitrary")),
        name="inproj",
    )(x, g, w)


def _lru_kernel(x_ref, p0_ref, p1_ref, p2_ref, h0_ref, cw_ref, cb_ref, wax_ref, bax_ref, lam_ref,
                proj_ref, ya_ref, n0_ref, n1_ref, n2_ref, ht_ref, xe_scr, a_scr, b_scr, hc_scr,
                *, nb, bt, first_pos_is_zero):
    c = pl.program_id(0)
    rows = nb * bt
    pad = 2 * SUBLANES

    @pl.when(c == 0)
    def _():
        n0_ref[...] = p0_ref[...]
        n1_ref[...] = p1_ref[...]
        n2_ref[...] = p2_ref[...]
        ht_ref[...] = h0_ref[...]
        if bt > 1:
            xe_scr[:pad - 3 * nb, :] = jnp.zeros((pad - 3 * nb, D_LRU), F32)

    nlam = -lam_ref[...]
    sp = jnp.maximum(nlam, 0.0) + jnp.log1p(jnp.exp(-jnp.abs(nlam)))

    if bt > 1:
        xe_scr[pad - 3 * nb:pad - 2 * nb, :] = n0_ref[...]
        xe_scr[pad - 2 * nb:pad - nb, :] = n1_ref[...]
        xe_scr[pad - nb:pad, :] = n2_ref[...]
        xe_scr[pad:, :] = x_ref[...]
        is_first = (lax.broadcasted_iota(jnp.int32, (rows, LRU_BW), 0) < nb) & (c == 0)

    for hb in range(H_LRU):
        cs = slice(hb * LRU_BW, (hb + 1) * LRU_BW)
        if bt > 1:
            xs = pltpu.roll(xe_scr[:, cs], nb, axis=0)
            xc = (cb_ref[:, cs] + xs[pad - 2 * nb:pad + rows - 2 * nb] * cw_ref[0:1, cs]
                  + xe_scr[pad - 2 * nb:pad + rows - 2 * nb, cs] * cw_ref[1:2, cs]
                  + xs[pad:] * cw_ref[2:3, cs]
                  + xe_scr[pad:, cs] * cw_ref[3:4, cs])
        else:
            xc = (cb_ref[:, cs] + n0_ref[:, cs] * cw_ref[0:1, cs] + n1_ref[:, cs] * cw_ref[1:2, cs]
                  + n2_ref[:, cs] * cw_ref[2:3, cs] + x_ref[:, cs] * cw_ref[3:4, cs])
        gates = jnp.dot(xc.astype(BF16), wax_ref[hb], preferred_element_type=F32) + bax_ref[hb]
        r = jax.nn.sigmoid(gates[:, :LRU_BW])
        i = jax.nn.sigmoid(gates[:, LRU_BW:])
        log_a = (-LRU_C) * r * sp[:, cs]
        a = jnp.exp(log_a)
        mult = jnp.sqrt(jnp.tanh(-log_a) * (1.0 + a * a))
        if first_pos_is_zero:
            mult = jnp.where(is_first, 1.0, mult)
        a_scr[:, cs] = a
        b_scr[:, cs] = mult * (i * xc)

    if bt > 1:
        n0_ref[...] = x_ref[rows - 3 * nb:rows - 2 * nb, :]
        n1_ref[...] = x_ref[rows - 2 * nb:rows - nb, :]
        n2_ref[...] = x_ref[rows - nb:, :]
    else:
        n0_ref[...] = n1_ref[...]
        n1_ref[...] = n2_ref[...]
        n2_ref[...] = x_ref[...]

    if bt == 1:
        h = a_scr[...] * ht_ref[...] + b_scr[...]
        b_scr[...] = h
        ht_ref[...] = h
    else:
        n_cb = D_LRU // LANES
        lo = _lo_rows((SUBLANES, LANES))
        hc_scr[:HALF, :] = ht_ref[...]
        hc_scr[HALF:, :] = ht_ref[...]

        def pair(k, hs):
            rs = pl.ds(pl.multiple_of(k * SUBLANES, SUBLANES), SUBLANES)
            out = []
            for cb in range(n_cb):
                cl = slice(cb * LANES, (cb + 1) * LANES)
                av, bv = a_scr[rs, cl], b_scr[rs, cl]
                first = av * hs[cb] + bv
                second = av * pltpu.roll(first, HALF, axis=0) + bv
                h = jnp.where(lo, first, second)
                b_scr[rs, cl] = h
                out.append(pltpu.roll(h, HALF, axis=0))
            return tuple(out)

        hs = tuple(hc_scr[:, cb * LANES:(cb + 1) * LANES] for cb in range(n_cb))
        hs = lax.fori_loop(0, rows // SUBLANES, pair, hs)
        for cb in range(n_cb):
            hc_scr[:, cb * LANES:(cb + 1) * LANES] = hs[cb]
        ht_ref[...] = hc_scr[:HALF, :]

    ya_ref[...] = jnp.dot(b_scr[...].astype(BF16), proj_ref[...], preferred_element_type=F32)


def _lru(z, conv0, h0, cw, cb, wax, bax, lam, proj, *, nb, seq_len, bt, first_pos_is_zero):
    assert bt == 1 or (nb == HALF and bt % 2 == 0 and bt >= 4)
    rows = nb * bt
    p0, p1, p2 = conv0[:, 0, :], conv0[:, 1, :], conv0[:, 2, :]
    st = pl.BlockSpec((nb, D_LRU), lambda c: (0, 0))
    st_sds = jax.ShapeDtypeStruct((nb, D_LRU), F32)
    tile = pl.BlockSpec((rows, D_LRU), lambda c: (c, 0))
    kern = functools.partial(_lru_kernel, nb=nb, bt=bt, first_pos_is_zero=first_pos_is_zero)
    xe_shape = (rows + 2 * SUBLANES, D_LRU) if bt > 1 else (SUBLANES, LANES)
    ya, n0, n1, n2, ht = pl.pallas_call(
        kern,
        grid=(seq_len // bt,),
        in_specs=[tile, st, st, st, st,
                  _const_spec((CONV_W, D_LRU)), _const_spec((1, D_LRU)),
                  _const_spec((H_LRU, LRU_BW, 2 * LRU_BW)), _const_spec((H_LRU, 1, 2 * LRU_BW)),
                  _const_spec((1, D_LRU)), _const_spec((D_LRU, D_MODEL))],
        out_specs=(tile, st, st, st, st),
        out_shape=(jax.ShapeDtypeStruct((nb * seq_len, D_LRU), F32), st_sds, st_sds, st_sds, st_sds),
        scratch_shapes=[pltpu.VMEM(xe_shape, F32), pltpu.VMEM((rows, D_LRU), F32),
                        pltpu.VMEM((rows, D_LRU), F32), pltpu.VMEM((SUBLANES, D_LRU), F32)],
        compiler_params=_params(("arbitrary",)),
        name="lru",
    )(z, p0, p1, p2, h0, cw, cb, wax, bax, lam, proj)
    return ya, jnp.stack([n0, n1, n2], axis=1), ht


def _gelu_tanh(x):
    c = math.sqrt(2.0 / math.pi)
    return 0.5 * x * (1.0 + jnp.tanh(c * (x + 0.044715 * (x * x * x))))


S5_SCAN_COLS = 8 * LANES


def _s5_kernel(u_ref, s0r_ref, s0i_ref, car_ref, cai_ref, wb_ref, wc_ref, d_ref, wv_ref, wg_ref,
               yb_ref, sr_ref, si_ref, xr_scr, xi_scr, pr_scr, pi_scr, y_scr, *, nb, bt):
    c = pl.program_id(0)
    rows = nb * bt

    @pl.when(c == 0)
    def _():
        sr_ref[...] = s0r_ref[...]
        si_ref[...] = s0i_ref[...]

    u = u_ref[...]
    ub = u.astype(BF16)
    if bt > 1:
        up = jnp.where(_lo_rows(u.shape), 0.0, pltpu.roll(u, HALF, axis=0)).astype(BF16)
    for j in range(S5_SLABS):
        js = slice(j * S5_SLAB_IN, (j + 1) * S5_SLAB_IN)
        if bt > 1:
            w = jnp.dot(jnp.concatenate([ub[:, js], up[:, js]], axis=1), wb_ref[j],
                        preferred_element_type=F32)
        else:
            w = jnp.dot(ub[:, js], wb_ref[j, :S5_SLAB_IN, :], preferred_element_type=F32)
        xr_scr[:, j * S5_SLAB_ST:(j + 1) * S5_SLAB_ST] = w[:, :S5_SLAB_ST]
        xi_scr[:, j * S5_SLAB_ST:(j + 1) * S5_SLAB_ST] = w[:, S5_SLAB_ST:]

    if bt == 1:
        ar, ai = car_ref[0:1, :], cai_ref[0:1, :]
        pr, pi = sr_ref[...], si_ref[...]
        nr = ar * pr - ai * pi + xr_scr[...]
        ni = ar * pi + ai * pr + xi_scr[...]
        xr_scr[...] = nr
        xi_scr[...] = ni
        sr_ref[...] = nr
        si_ref[...] = ni
    else:
        lo = _lo_rows((SUBLANES, LANES))
        pr_scr[:HALF, :] = sr_ref[...]
        pr_scr[HALF:, :] = sr_ref[...]
        pi_scr[:HALF, :] = si_ref[...]
        pi_scr[HALF:, :] = si_ref[...]
        n_cb = S5_SCAN_COLS // LANES
        for grp in range(S5_N // S5_SCAN_COLS):
            cols = [slice(grp * S5_SCAN_COLS + cb * LANES, grp * S5_SCAN_COLS + (cb + 1) * LANES)
                    for cb in range(n_cb)]

            def pair(k, carry, cols=cols):
                rs = pl.ds(pl.multiple_of(k * SUBLANES, SUBLANES), SUBLANES)
                out = []
                for cb, cl in enumerate(cols):
                    pr, pi = carry[2 * cb], carry[2 * cb + 1]
                    ar, ai = car_ref[:, cl], cai_ref[:, cl]
                    nr = ar * pr - ai * pi + xr_scr[rs, cl]
                    ni = ar * pi + ai * pr + xi_scr[rs, cl]
                    xr_scr[rs, cl] = nr
                    xi_scr[rs, cl] = ni
                    out.append(jnp.where(lo, pltpu.roll(nr, HALF, axis=0), nr))
                    out.append(jnp.where(lo, pltpu.roll(ni, HALF, axis=0), ni))
                return tuple(out)

            carry = []
            for cl in cols:
                carry += [pr_scr[:, cl], pi_scr[:, cl]]
            carry = lax.fori_loop(0, rows // SUBLANES, pair, tuple(carry))
            for cb, cl in enumerate(cols):
                pr_scr[:, cl] = carry[2 * cb]
                pi_scr[:, cl] = carry[2 * cb + 1]
        sr_ref[...] = pr_scr[:HALF, :]
        si_ref[...] = pi_scr[:HALF, :]

    for j in range(S5_SLABS):
        ss = slice(j * S5_SLAB_ST, (j + 1) * S5_SLAB_ST)
        xs = jnp.concatenate([xr_scr[:, ss].astype(BF16), xi_scr[:, ss].astype(BF16)], axis=1)
        y_scr[:, j * S5_SLAB_IN:(j + 1) * S5_SLAB_IN] = jnp.dot(
            xs, wc_ref[j], preferred_element_type=F32)
    y = y_scr[...] + d_ref[...] * u
    v = _gelu_tanh(y).astype(BF16)
    yv = jnp.dot(v, wv_ref[...], preferred_element_type=F32)
    yg = jnp.dot(v, wg_ref[...], preferred_element_type=F32)
    yb_ref[...] = yv * jax.nn.sigmoid(yg)


def _s5(z, s0r, s0i, ca_re, ca_im, wb, wc, d, wv, wg, *, nb, seq_len, bt):
    assert bt == 1 or (nb == HALF and bt % 2 == 0)
    rows = nb * bt
    st = pl.BlockSpec((nb, S5_N), lambda c: (0, 0))
    st_sds = jax.ShapeDtypeStruct((nb, S5_N), F32)
    x_s5_block = (D_LRU + 2 * D_MODEL) // D_S5
    yb, sr, si = pl.pallas_call(
        functools.partial(_s5_kernel, nb=nb, bt=bt),
        grid=(seq_len // bt,),
        in_specs=[pl.BlockSpec((rows, D_S5), lambda c: (c, x_s5_block)), st, st,
                  _const_spec((SUBLANES, S5_N)), _const_spec((SUBLANES, S5_N)),
                  _const_spec((S5_SLABS, 2 * S5_SLAB_IN, 2 * S5_SLAB_ST)),
                  _const_spec((S5_SLABS, 2 * S5_SLAB_ST, S5_SLAB_IN)),
                  _const_spec((1, D_S5)), _const_spec((D_S5, D_MODEL)), _const_spec((D_S5, D_MODEL))],
        out_specs=(pl.BlockSpec((rows, D_MODEL), lambda c: (c, 0)), st, st),
        out_shape=(jax.ShapeDtypeStruct((nb * seq_len, D_MODEL), F32), st_sds, st_sds),
        scratch_shapes=[pltpu.VMEM((rows, S5_N), F32), pltpu.VMEM((rows, S5_N), F32),
                        pltpu.VMEM((SUBLANES, S5_N), F32), pltpu.VMEM((SUBLANES, S5_N), F32),
                        pltpu.VMEM((rows, D_S5), F32)],
        compiler_params=_params(("arbitrary",)),
        name="s5",
    )(z, s0r, s0i, ca_re, ca_im, wb, wc, d, wv, wg)
    return yb, sr, si


def _merge_kernel(ya_ref, yb_ref, ga_ref, gb_ref, x_ref, w_ref, g_ref, x1_ref, u2_ref):
    merged = jax.nn.sigmoid(ga_ref[...]) * ya_ref[...] + jax.nn.sigmoid(gb_ref[...]) * yb_ref[...]
    x1 = x_ref[...] + jnp.dot(merged.astype(BF16), w_ref[...], preferred_element_type=F32)
    x1_ref[...] = x1
    u2_ref[...] = _rmsnorm(x1, g_ref[...]).astype(BF16)


def _merge(ya, yb, z, x, w_out, g, bm):
    m = x.shape[0]
    row = lambda col: pl.BlockSpec((bm, D_MODEL), lambda i: (i, col))
    return pl.pallas_call(
        _merge_kernel,
        grid=(m // bm,),
        in_specs=[row(0), row(0), row(1), row(2), row(0),
                  _const_spec((D_MODEL, D_MODEL)), _const_spec((1, D_MODEL))],
        out_specs=(row(0), row(0)),
        out_shape=(jax.ShapeDtypeStruct((m, D_MODEL), F32), jax.ShapeDtypeStruct((m, D_MODEL), BF16)),
        compiler_params=_params(("parallel",)),
        name="merge",
    )(ya, yb, z, z, x, w_out, g)


def _ffn_kernel(u_ref, x1_ref, wg_ref, wu_ref, wd_ref, g_ref, y_ref, acc_scr):
    f = pl.program_id(1)

    @pl.when(f == 0)
    def _():
        acc_scr[...] = x1_ref[...]

    u = u_ref[...]
    hg = jnp.dot(u, wg_ref[...], preferred_element_type=F32)
    hu = jnp.dot(u, wu_ref[...], preferred_element_type=F32)
    hid = (hg * jax.nn.sigmoid(hg) * hu).astype(BF16)
    acc_scr[...] += jnp.dot(hid, wd_ref[...], preferred_element_type=F32)

    @pl.when(f == pl.num_programs(1) - 1)
    def _():
        y_ref[...] = _rmsnorm(acc_scr[...], g_ref[...])


def _ffn(u2, x1, wg, wu, wd, g, bm, bf):
    m = x1.shape[0]
    return pl.pallas_call(
        _ffn_kernel,
        grid=(m // bm, D_FF // bf),
        in_specs=[pl.BlockSpec((bm, D_MODEL), lambda i, f: (i, 0)),
                  pl.BlockSpec((bm, D_MODEL), lambda i, f: (i, 0)),
                  pl.BlockSpec((D_MODEL, bf), lambda i, f: (0, f)),
                  pl.BlockSpec((D_MODEL, bf), lambda i, f: (0, f)),
                  pl.BlockSpec((bf, D_MODEL), lambda i, f: (f, 0)),
                  pl.BlockSpec((1, D_MODEL), lambda i, f: (0, 0))],
        out_specs=pl.BlockSpec((bm, D_MODEL), lambda i, f: (i, 0)),
        out_shape=jax.ShapeDtypeStruct((m, D_MODEL), F32),
        scratch_shapes=[pltpu.VMEM((bm, D_MODEL), F32)],
        compiler_params=_params(("parallel", "arbitrary")),
        name="ffn",
    )(u2, x1, wg, wu, wd, g)


def _layer(x, conv0, h0, s0r, s0i, w, *, nb, seq_len, bt_lru, bt_s5, bm, bm_merge, bn, bf,
           first_pos_is_zero):
    z = _inproj(x, w['norm_mix_g'], w['w_in'], bm, bn)
    ya, conv_n, h_n = _lru(z, conv0, h0, w['cw'], w['cb'], w['wax'], w['bax'], w['lam'], w['proj'],
                           nb=nb, seq_len=seq_len, bt=bt_lru, first_pos_is_zero=first_pos_is_zero)
    yb, sr, si = _s5(z, s0r, s0i, w['ca_re'], w['ca_im'], w['wb'], w['wc'], w['d'], w['wv'], w['wg'],
                     nb=nb, seq_len=seq_len, bt=bt_s5)
    x1, u2 = _merge(ya, yb, z, x, w['w_out'], w['norm_ffn_g'], bm_merge)
    y = _ffn(u2, x1, w['ffn_wg'], w['ffn_wu'], w['ffn_wd'], w['norm_final_g'], bm, bf)
    return y, conv_n, h_n, sr, si


def kernel(x_prompt, x_sample, state_lru_conv, state_lru_h, state_s5_re, state_s5_im, norm_mix_g, w_in, lru_conv_w, lru_conv_b, lru_wa, lru_ba, lru_wx, lru_bx, lru_lambda, lru_proj, s5_lambda_re, s5_lambda_im, s5_log_dt, s5_b_re, s5_b_im, s5_c_re, s5_c_im, s5_d, s5_glu_wv, s5_glu_wg, w_out, norm_ffn_g, ffn_w_gate, ffn_w_up, ffn_w_down, norm_final_g):
    batch, seq, _ = x_prompt.shape
    dec_batch = x_sample.shape[0]
    l = 0
    ca_re, ca_im, wb, wc = _s5_prepare(s5_lambda_re[l], s5_lambda_im[l], s5_log_dt[l],
                                       s5_b_re[l], s5_b_im[l], s5_c_re[l], s5_c_im[l])
    wi = w_in[l]
    wi = jnp.concatenate([wi[:, :D_LRU], wi[:, D_LRU + D_S5:], wi[:, D_LRU:D_LRU + D_S5]], axis=1)
    w = dict(
        norm_mix_g=norm_mix_g[l][None, :], w_in=wi.astype(BF16),
        cw=lru_conv_w[l], cb=lru_conv_b[l][None, :],
        wax=jnp.concatenate([lru_wa[l], lru_wx[l]], axis=-1).astype(BF16),
        bax=jnp.concatenate([lru_ba[l], lru_bx[l]], axis=-1)[:, None, :],
        lam=lru_lambda[l][None, :], proj=lru_proj[l].astype(BF16),
        ca_re=ca_re, ca_im=ca_im, wb=wb, wc=wc, d=s5_d[l].reshape(1, D_S5),
        wv=s5_glu_wv[l].astype(BF16), wg=s5_glu_wg[l].astype(BF16),
        w_out=w_out[l].astype(BF16), norm_ffn_g=norm_ffn_g[l][None, :],
        ffn_wg=ffn_w_gate[l].astype(BF16), ffn_wu=ffn_w_up[l].astype(BF16),
        ffn_wd=ffn_w_down[l].astype(BF16), norm_final_g=norm_final_g[None, :],
    )
    zeros = lambda *s: jnp.zeros(s, F32)
    xp = jnp.swapaxes(x_prompt, 0, 1).reshape(seq * batch, D_MODEL)
    yp, conv_p, h_p, sr_p, si_p = _layer(
        xp, zeros(batch, CONV_W - 1, D_LRU), zeros(batch, D_LRU),
        zeros(batch, S5_N), zeros(batch, S5_N), w,
        nb=batch, seq_len=seq, bt_lru=128, bt_s5=64, bm=512, bm_merge=256, bn=1024, bf=512,
        first_pos_is_zero=True)
    yp = jnp.swapaxes(yp.reshape(seq, batch, D_MODEL), 0, 1)
    ys, conv_s, h_s, sr_s, si_s = _layer(
        x_sample.reshape(dec_batch, D_MODEL), state_lru_conv[l], state_lru_h[l],
        state_s5_re[l].reshape(dec_batch, S5_N), state_s5_im[l].reshape(dec_batch, S5_N), w,
        nb=dec_batch, seq_len=1, bt_lru=1, bt_s5=1, bm=dec_batch, bm_merge=dec_batch, bn=1024, bf=512,
        first_pos_is_zero=False)
    st = lambda a, n: a.reshape(1, n, S5_G, S5_P)
    return (yp, ys.reshape(dec_batch, 1, D_MODEL),
            conv_p[None], h_p[None], st(sr_p, batch), st(si_p, batch),
            conv_s[None], h_s[None], st(sr_s, dec_batch), st(si_s, dec_batch))
```

```python
import functools
import math

import jax
import jax.numpy as jnp
from jax import lax
from jax.experimental import pallas as pl
from jax.experimental.pallas import tpu as pltpu

F32 = jnp.float32
BF16 = jnp.bfloat16

D_MODEL = 2048
D_LRU = D_MODEL
H_LRU = 8
LRU_BW = D_LRU // H_LRU
LRU_C = 8.0
CONV_W = 4
D_S5 = D_MODEL // 2
S5_H = 16
S5_G = D_S5 // S5_H
S5_P = 64
S5_N = S5_G * S5_P
D_FF = 5632
IN_WIDTH = D_LRU + D_S5 + 2 * D_MODEL
EPS = 1e-6

S5_SLAB_G = 8
S5_SLABS = S5_G // S5_SLAB_G
S5_SLAB_IN = S5_SLAB_G * S5_H
S5_SLAB_ST = S5_SLAB_G * S5_P

LANES = 128
SUBLANES = 8
HALF = SUBLANES // 2
VMEM_LIMIT_BYTES = 56 * 1024 * 1024


def _params(sem):
    return pltpu.CompilerParams(dimension_semantics=sem, vmem_limit_bytes=VMEM_LIMIT_BYTES)


def _const_spec(shape):
    nd = len(shape)
    return pl.BlockSpec(shape, lambda *_: (0,) * nd, pipeline_mode=pl.Buffered(1))


def _rmsnorm(x, g):
    ms = jnp.mean(x * x, axis=-1, keepdims=True)
    return x * lax.rsqrt(ms + EPS) * g


def _lo_rows(shape):
    return lax.broadcasted_iota(jnp.int32, shape, 0) % SUBLANES < HALF


def _s5prep_kernel(lre_ref, lim_ref, ldt_ref, br_ref, bi_ref,
                   bbr_ref, bbi_ref, abbr_ref, abbi_ref, abr_ref, abi_ref, a2r_ref, a2i_ref):
    lre = lre_ref[...]
    lim = lim_ref[...]
    dt = jnp.exp(ldt_ref[...])
    mag = jnp.exp(lre * dt)
    ab_re = mag * jnp.cos(lim * dt)
    ab_im = mag * jnp.sin(lim * dt)
    e_re = ab_re - 1.0
    e_im = ab_im
    den = lre * lre + lim * lim
    co_re = (e_re * lre + e_im * lim) / den
    co_im = (e_im * lre - e_re * lim) / den
    br = br_ref[...]
    bi = bi_ref[...]
    bb_re = co_re * br - co_im * bi
    bb_im = co_re * bi + co_im * br
    bbr_ref[...] = bb_re
    bbi_ref[...] = bb_im
    abbr_ref[...] = ab_re * bb_re - ab_im * bb_im
    abbi_ref[...] = ab_re * bb_im + ab_im * bb_re
    abr_ref[...] = ab_re
    abi_ref[...] = ab_im
    a2r_ref[...] = ab_re * ab_re - ab_im * ab_im
    a2i_ref[...] = 2.0 * (ab_re * ab_im)


def _s5_prepare(lam_re, lam_im, log_dt, b_re, b_im, c_re, c_im):
    full = (S5_G, S5_P, S5_H)
    flat = (S5_G * S5_P * S5_H // LANES, LANES)
    bc = lambda a: jnp.broadcast_to(a, full).reshape(flat)
    args = (bc(lam_re[:, :, None]), bc(lam_im[:, :, None]), bc(log_dt[:, None, None]),
            b_re.reshape(flat), b_im.reshape(flat))
    sds = jax.ShapeDtypeStruct(flat, F32)
    bbr, bbi, abbr, abbi, abr, abi, a2r, a2i = pl.pallas_call(
        _s5prep_kernel, out_shape=(sds,) * 8, name="s5prep")(*args)
    per_state = lambda a: a.reshape(full)[:, :, 0].reshape(1, S5_N)
    coeff = lambda a1, a2: jnp.concatenate(
        [jnp.broadcast_to(per_state(a1), (HALF, S5_N)), jnp.broadcast_to(per_state(a2), (HALF, S5_N))], 0)
    ca_re, ca_im = coeff(abr, a2r), coeff(abi, a2i)
    eye = jnp.eye(S5_SLAB_G, dtype=F32)

    def slab_b(re, im):
        b = jnp.stack([re.reshape(full), im.reshape(full)], 0)
        b = b.reshape(2, S5_SLABS, S5_SLAB_G, S5_P, S5_H)
        return jnp.einsum('cjgph,gk->jghckp', b, eye).reshape(S5_SLABS, S5_SLAB_IN, 2 * S5_SLAB_ST)

    wb = jnp.concatenate([slab_b(bbr, bbi), slab_b(abbr, abbi)], axis=1)
    cc = jnp.stack([c_re, -c_im], 0).reshape(2, S5_SLABS, S5_SLAB_G, S5_H, S5_P)
    wc = jnp.einsum('cjghp,gk->jcgpkh', cc, eye).reshape(S5_SLABS, 2 * S5_SLAB_ST, S5_SLAB_IN)
    return ca_re, ca_im, wb.astype(BF16), wc.astype(BF16)


PERM_T = 64


def _perm_matrix(nb, to_time_major):
    n = nb * PERM_T
    tm = jnp.arange(n)
    sm = (tm % nb) * PERM_T + tm // nb
    p = jax.nn.one_hot(sm, n, dtype=BF16)
    return p if to_time_major else p.T


def _row_tile_spec(nb, bt, width):
    if bt == 1:
        return pl.BlockSpec((nb, width), lambda *g: (0, 0))
    return pl.BlockSpec((nb, bt, width), lambda *g: (0, g[0], 0))


N_LRU_BLOCKS = 2
INPROJ_BN = 1024
ZR_WIDTH = IN_WIDTH - D_LRU


def _zr_block(j):
    return jnp.where(j <= N_LRU_BLOCKS, (ZR_WIDTH - D_S5) // INPROJ_BN, j - N_LRU_BLOCKS - 1)


def _inproj_kernel(x_ref, g_ref, perm_ref, w_ref, zl_ref, zr_ref, u_scr, *, nb, bt):
    j = pl.program_id(1)

    @pl.when(j == 0)
    def _():
        if bt == 1:
            u_scr[...] = _rmsnorm(x_ref[...], g_ref[...]).astype(BF16)
        else:
            for s in range(bt // PERM_T):
                ts = slice(s * PERM_T, (s + 1) * PERM_T)
                xin = jnp.concatenate([x_ref[b, ts, :] for b in range(nb)], axis=0)
                u = _rmsnorm(xin, g_ref[...]).astype(BF16)
                u_scr[s * nb * PERM_T:(s + 1) * nb * PERM_T, :] = jnp.dot(
                    perm_ref[...], u, preferred_element_type=F32).astype(BF16)

    acc = jnp.dot(u_scr[...], w_ref[...], preferred_element_type=F32)

    @pl.when(j < N_LRU_BLOCKS)
    def _():
        zl_ref[...] = acc

    @pl.when(j >= N_LRU_BLOCKS)
    def _():
        zr_ref[...] = acc.astype(BF16)


def _inproj(x, g, w, *, nb, seq_len, bt):
    assert bt == 1 or bt % PERM_T == 0
    m, bm, bn = nb * seq_len, nb * bt, INPROJ_BN
    perm = _perm_matrix(nb, True) if bt > 1 else jnp.zeros((SUBLANES, LANES), BF16)
    return pl.pallas_call(
        functools.partial(_inproj_kernel, nb=nb, bt=bt),
        grid=(m // bm, IN_WIDTH // bn),
        in_specs=[_row_tile_spec(nb, bt, D_MODEL),
                  pl.BlockSpec((1, D_MODEL), lambda i, j: (0, 0)),
                  _const_spec(perm.shape),
                  pl.BlockSpec((D_MODEL, bn), lambda i, j: (0, j))],
        out_specs=(pl.BlockSpec((bm, bn), lambda i, j: (i, jnp.minimum(j, N_LRU_BLOCKS - 1))),
                   pl.BlockSpec((bm, bn), lambda i, j: (i, _zr_block(j)))),
        out_shape=(jax.ShapeDtypeStruct((m, D_LRU), F32), jax.ShapeDtypeStruct((m, ZR_WIDTH), BF16)),
        scratch_shapes=[pltpu.VMEM((bm, D_MODEL), BF16)],
        compiler_params=_params(("parallel", "arbitrary")),
        name="inproj",
    )(x, g, perm, w)


def _lru_kernel(x_ref, p0_ref, p1_ref, p2_ref, h0_ref, cw_ref, cb_ref, wax_ref, bax_ref, lam_ref,
                proj_ref, ya_ref, n0_ref, n1_ref, n2_ref, ht_ref, xe_scr, a_scr, b_scr, hc_scr,
                *, nb, bt, first_pos_is_zero):
    c = pl.program_id(0)
    rows = nb * bt
    pad = 2 * SUBLANES

    @pl.when(c == 0)
    def _():
        n0_ref[...] = p0_ref[...]
        n1_ref[...] = p1_ref[...]
        n2_ref[...] = p2_ref[...]
        ht_ref[...] = h0_ref[...]
        if bt > 1:
            xe_scr[:pad - 3 * nb, :] = jnp.zeros((pad - 3 * nb, D_LRU), F32)

    nlam = -lam_ref[...]
    sp = jnp.maximum(nlam, 0.0) + jnp.log1p(jnp.exp(-jnp.abs(nlam)))

    if bt > 1:
        xe_scr[pad - 3 * nb:pad - 2 * nb, :] = n0_ref[...]
        xe_scr[pad - 2 * nb:pad - nb, :] = n1_ref[...]
        xe_scr[pad - nb:pad, :] = n2_ref[...]
        xe_scr[pad:, :] = x_ref[...]
        is_first = (lax.broadcasted_iota(jnp.int32, (rows, LRU_BW), 0) < nb) & (c == 0)

    for hb in range(H_LRU):
        cs = slice(hb * LRU_BW, (hb + 1) * LRU_BW)
        if bt > 1:
            xs = pltpu.roll(xe_scr[:, cs], nb, axis=0)
            xc = (cb_ref[:, cs] + xs[pad - 2 * nb:pad + rows - 2 * nb] * cw_ref[0:1, cs]
                  + xe_scr[pad - 2 * nb:pad + rows - 2 * nb, cs] * cw_ref[1:2, cs]
                  + xs[pad:] * cw_ref[2:3, cs]
                  + xe_scr[pad:, cs] * cw_ref[3:4, cs])
        else:
            xc = (cb_ref[:, cs] + n0_ref[:, cs] * cw_ref[0:1, cs] + n1_ref[:, cs] * cw_ref[1:2, cs]
                  + n2_ref[:, cs] * cw_ref[2:3, cs] + x_ref[:, cs] * cw_ref[3:4, cs])
        gates = jnp.dot(xc.astype(BF16), wax_ref[hb], preferred_element_type=F32) + bax_ref[hb]
        r = jax.nn.sigmoid(gates[:, :LRU_BW])
        i = jax.nn.sigmoid(gates[:, LRU_BW:])
        log_a = (-LRU_C) * r * sp[:, cs]
        a = jnp.exp(log_a)
        mult = jnp.sqrt(jnp.tanh(-log_a) * (1.0 + a * a))
        if first_pos_is_zero:
            mult = jnp.where(is_first, 1.0, mult)
        a_scr[:, cs] = a
        b_scr[:, cs] = mult * (i * xc)

    if bt > 1:
        n0_ref[...] = x_ref[rows - 3 * nb:rows - 2 * nb, :]
        n1_ref[...] = x_ref[rows - 2 * nb:rows - nb, :]
        n2_ref[...] = x_ref[rows - nb:, :]
    else:
        n0_ref[...] = n1_ref[...]
        n1_ref[...] = n2_ref[...]
        n2_ref[...] = x_ref[...]

    if bt == 1:
        h = a_scr[...] * ht_ref[...] + b_scr[...]
        b_scr[...] = h
        ht_ref[...] = h
    else:
        n_cb = D_LRU // LANES
        lo = _lo_rows((SUBLANES, LANES))
        hc_scr[:HALF, :] = ht_ref[...]
        hc_scr[HALF:, :] = ht_ref[...]

        def pair(k, hs):
            rs = pl.ds(pl.multiple_of(k * SUBLANES, SUBLANES), SUBLANES)
            out = []
            for cb in range(n_cb):
                cl = slice(cb * LANES, (cb + 1) * LANES)
                av, bv = a_scr[rs, cl], b_scr[rs, cl]
                first = av * hs[cb] + bv
                second = av * pltpu.roll(first, HALF, axis=0) + bv
                h = jnp.where(lo, first, second)
                b_scr[rs, cl] = h
                out.append(pltpu.roll(h, HALF, axis=0))
            return tuple(out)

        hs = tuple(hc_scr[:, cb * LANES:(cb + 1) * LANES] for cb in range(n_cb))
        hs = lax.fori_loop(0, rows // SUBLANES, pair, hs)
        for cb in range(n_cb):
            hc_scr[:, cb * LANES:(cb + 1) * LANES] = hs[cb]
        ht_ref[...] = hc_scr[:HALF, :]

    ya = jnp.dot(b_scr[...].astype(BF16), proj_ref[...], preferred_element_type=F32)
    ya_ref[...] = ya.astype(BF16)


def _lru(z, conv0, h0, cw, cb, wax, bax, lam, proj, *, nb, seq_len, bt, first_pos_is_zero):
    assert bt == 1 or (nb == HALF and bt % 2 == 0 and bt >= 4)
    rows = nb * bt
    p0, p1, p2 = conv0[:, 0, :], conv0[:, 1, :], conv0[:, 2, :]
    st = pl.BlockSpec((nb, D_LRU), lambda c: (0, 0))
    st_sds = jax.ShapeDtypeStruct((nb, D_LRU), F32)
    tile = pl.BlockSpec((rows, D_LRU), lambda c: (c, 0))
    kern = functools.partial(_lru_kernel, nb=nb, bt=bt, first_pos_is_zero=first_pos_is_zero)
    xe_shape = (rows + 2 * SUBLANES, D_LRU) if bt > 1 else (SUBLANES, LANES)
    ya, n0, n1, n2, ht = pl.pallas_call(
        kern,
        grid=(seq_len // bt,),
        in_specs=[tile, st, st, st, st,
                  _const_spec((CONV_W, D_LRU)), _const_spec((1, D_LRU)),
                  _const_spec((H_LRU, LRU_BW, 2 * LRU_BW)), _const_spec((H_LRU, 1, 2 * LRU_BW)),
                  _const_spec((1, D_LRU)), _const_spec((D_LRU, D_MODEL))],
        out_specs=(tile, st, st, st, st),
        out_shape=(jax.ShapeDtypeStruct((nb * seq_len, D_LRU), BF16), st_sds, st_sds, st_sds, st_sds),
        scratch_shapes=[pltpu.VMEM(xe_shape, F32), pltpu.VMEM((rows, D_LRU), F32),
                        pltpu.VMEM((rows, D_LRU), F32), pltpu.VMEM((SUBLANES, D_LRU), F32)],
        compiler_params=_params(("arbitrary",)),
        name="lru",
    )(z, p0, p1, p2, h0, cw, cb, wax, bax, lam, proj)
    return ya, jnp.stack([n0, n1, n2], axis=1), ht


def _gelu_tanh(x):
    c = math.sqrt(2.0 / math.pi)
    return 0.5 * x * (1.0 + jnp.tanh(c * (x + 0.044715 * (x * x * x))))


S5_SCAN_COLS = 8 * LANES


def _s5_kernel(u_ref, s0r_ref, s0i_ref, car_ref, cai_ref, wb_ref, wc_ref, d_ref, wv_ref, wg_ref,
               yb_ref, sr_ref, si_ref, xr_scr, xi_scr, pr_scr, pi_scr, y_scr, *, nb, bt):
    c = pl.program_id(0)
    rows = nb * bt

    @pl.when(c == 0)
    def _():
        sr_ref[...] = s0r_ref[...]
        si_ref[...] = s0i_ref[...]

    ub = u_ref[...]
    u = ub.astype(F32)
    if bt > 1:
        up = jnp.where(_lo_rows(u.shape), 0.0, pltpu.roll(u, HALF, axis=0)).astype(BF16)
    for j in range(S5_SLABS):
        js = slice(j * S5_SLAB_IN, (j + 1) * S5_SLAB_IN)
        if bt > 1:
            w = jnp.dot(jnp.concatenate([ub[:, js], up[:, js]], axis=1), wb_ref[j],
                        preferred_element_type=F32)
        else:
            w = jnp.dot(ub[:, js], wb_ref[j, :S5_SLAB_IN, :], preferred_element_type=F32)
        xr_scr[:, j * S5_SLAB_ST:(j + 1) * S5_SLAB_ST] = w[:, :S5_SLAB_ST]
        xi_scr[:, j * S5_SLAB_ST:(j + 1) * S5_SLAB_ST] = w[:, S5_SLAB_ST:]

    if bt == 1:
        ar, ai = car_ref[0:1, :], cai_ref[0:1, :]
        pr, pi = sr_ref[...], si_ref[...]
        nr = ar * pr - ai * pi + xr_scr[...]
        ni = ar * pi + ai * pr + xi_scr[...]
        xr_scr[...] = nr
        xi_scr[...] = ni
        sr_ref[...] = nr
        si_ref[...] = ni
    else:
        lo = _lo_rows((SUBLANES, LANES))
        pr_scr[:HALF, :] = sr_ref[...]
        pr_scr[HALF:, :] = sr_ref[...]
        pi_scr[:HALF, :] = si_ref[...]
        pi_scr[HALF:, :] = si_ref[...]
        n_cb = S5_SCAN_COLS // LANES
        for grp in range(S5_N // S5_SCAN_COLS):
            cols = [slice(grp * S5_SCAN_COLS + cb * LANES, grp * S5_SCAN_COLS + (cb + 1) * LANES)
                    for cb in range(n_cb)]

            def pair(k, carry, cols=cols):
                rs = pl.ds(pl.multiple_of(k * SUBLANES, SUBLANES), SUBLANES)
                out = []
                for cb, cl in enumerate(cols):
                    pr, pi = carry[2 * cb], carry[2 * cb + 1]
                    ar, ai = car_ref[:, cl], cai_ref[:, cl]
                    nr = ar * pr - ai * pi + xr_scr[rs, cl]
                    ni = ar * pi + ai * pr + xi_scr[rs, cl]
                    xr_scr[rs, cl] = nr
                    xi_scr[rs, cl] = ni
                    out.append(jnp.where(lo, pltpu.roll(nr, HALF, axis=0), nr))
                    out.append(jnp.where(lo, pltpu.roll(ni, HALF, axis=0), ni))
                return tuple(out)

            carry = []
            for cl in cols:
                carry += [pr_scr[:, cl], pi_scr[:, cl]]
            carry = lax.fori_loop(0, rows // SUBLANES, pair, tuple(carry))
            for cb, cl in enumerate(cols):
                pr_scr[:, cl] = carry[2 * cb]
                pi_scr[:, cl] = carry[2 * cb + 1]
        sr_ref[...] = pr_scr[:HALF, :]
        si_ref[...] = pi_scr[:HALF, :]

    for j in range(S5_SLABS):
        ss = slice(j * S5_SLAB_ST, (j + 1) * S5_SLAB_ST)
        xs = jnp.concatenate([xr_scr[:, ss].astype(BF16), xi_scr[:, ss].astype(BF16)], axis=1)
        y_scr[:, j * S5_SLAB_IN:(j + 1) * S5_SLAB_IN] = jnp.dot(
            xs, wc_ref[j], preferred_element_type=F32)
    y = y_scr[...] + d_ref[...] * u
    v = _gelu_tanh(y).astype(BF16)
    yv = jnp.dot(v, wv_ref[...], preferred_element_type=F32)
    yg = jnp.dot(v, wg_ref[...], preferred_element_type=F32)
    yb_ref[...] = (yv * jax.nn.sigmoid(yg)).astype(BF16)


def _s5(z, s0r, s0i, ca_re, ca_im, wb, wc, d, wv, wg, *, nb, seq_len, bt):
    assert bt == 1 or (nb == HALF and bt % 2 == 0)
    rows = nb * bt
    st = pl.BlockSpec((nb, S5_N), lambda c: (0, 0))
    st_sds = jax.ShapeDtypeStruct((nb, S5_N), F32)
    x_s5_block = (ZR_WIDTH - D_S5) // D_S5
    yb, sr, si = pl.pallas_call(
        functools.partial(_s5_kernel, nb=nb, bt=bt),
        grid=(seq_len // bt,),
        in_specs=[pl.BlockSpec((rows, D_S5), lambda c: (c, x_s5_block)), st, st,
                  _const_spec((SUBLANES, S5_N)), _const_spec((SUBLANES, S5_N)),
                  _const_spec((S5_SLABS, 2 * S5_SLAB_IN, 2 * S5_SLAB_ST)),
                  _const_spec((S5_SLABS, 2 * S5_SLAB_ST, S5_SLAB_IN)),
                  _const_spec((1, D_S5)), _const_spec((D_S5, D_MODEL)), _const_spec((D_S5, D_MODEL))],
        out_specs=(pl.BlockSpec((rows, D_MODEL), lambda c: (c, 0)), st, st),
        out_shape=(jax.ShapeDtypeStruct((nb * seq_len, D_MODEL), BF16), st_sds, st_sds),
        scratch_shapes=[pltpu.VMEM((rows, S5_N), F32), pltpu.VMEM((rows, S5_N), F32),
                        pltpu.VMEM((SUBLANES, S5_N), F32), pltpu.VMEM((SUBLANES, S5_N), F32),
                        pltpu.VMEM((rows, D_S5), F32)],
        compiler_params=_params(("arbitrary",)),
        name="s5",
    )(z, s0r, s0i, ca_re, ca_im, wb, wc, d, wv, wg)
    return yb, sr, si


def _merge_kernel(ya_ref, yb_ref, ga_ref, gb_ref, x_ref, perm_ref, w_ref, g_ref, x1_ref, u2_ref,
                  *, nb, bt):
    def gated(rs):
        f32 = lambda ref: ref[rs, :].astype(F32)
        merged = jax.nn.sigmoid(f32(ga_ref)) * f32(ya_ref) + jax.nn.sigmoid(f32(gb_ref)) * f32(yb_ref)
        return merged.astype(BF16)

    if bt == 1:
        x1 = x_ref[...] + jnp.dot(gated(slice(None)), w_ref[...], preferred_element_type=F32)
        x1_ref[...] = x1
        u2_ref[...] = _rmsnorm(x1, g_ref[...]).astype(BF16)
        return
    for s in range(bt // PERM_T):
        ts = slice(s * PERM_T, (s + 1) * PERM_T)
        mb = jnp.dot(perm_ref[...], gated(slice(s * nb * PERM_T, (s + 1) * nb * PERM_T)),
                     preferred_element_type=F32).astype(BF16)
        delta = jnp.dot(mb, w_ref[...], preferred_element_type=F32)
        for b in range(nb):
            x1 = x_ref[b, ts, :] + delta[b * PERM_T:(b + 1) * PERM_T]
            x1_ref[b, ts, :] = x1
            u2_ref[b, ts, :] = _rmsnorm(x1, g_ref[...]).astype(BF16)


def _merge(ya, yb, zr, x, w_out, g, *, nb, seq_len, bt):
    assert bt == 1 or bt % PERM_T == 0
    bm = nb * bt
    row = lambda col: pl.BlockSpec((bm, D_MODEL), lambda i: (i, col))
    perm = _perm_matrix(nb, False) if bt > 1 else jnp.zeros((SUBLANES, LANES), BF16)
    tile = _row_tile_spec(nb, bt, D_MODEL)
    return pl.pallas_call(
        functools.partial(_merge_kernel, nb=nb, bt=bt),
        grid=(seq_len // bt,),
        in_specs=[row(0), row(0), row(0), row(1), tile, _const_spec(perm.shape),
                  _const_spec((D_MODEL, D_MODEL)), _const_spec((1, D_MODEL))],
        out_specs=(tile, tile),
        out_shape=(jax.ShapeDtypeStruct(x.shape, F32), jax.ShapeDtypeStruct(x.shape, BF16)),
        compiler_params=_params(("parallel",)),
        name="merge",
    )(ya, yb, zr, zr, x, perm, w_out, g)


def _ffn_kernel(u_ref, x1_ref, wg_ref, wu_ref, wd_ref, g_ref, y_ref, acc_scr):
    f = pl.program_id(1)

    @pl.when(f == 0)
    def _():
        acc_scr[...] = x1_ref[...]

    u = u_ref[...]
    hg = jnp.dot(u, wg_ref[...], preferred_element_type=F32)
    hu = jnp.dot(u, wu_ref[...], preferred_element_type=F32)
    hid = (hg * jax.nn.sigmoid(hg) * hu).astype(BF16)
    acc_scr[...] += jnp.dot(hid, wd_ref[...], preferred_element_type=F32)

    @pl.when(f == pl.num_programs(1) - 1)
    def _():
        y_ref[...] = _rmsnorm(acc_scr[...], g_ref[...])


def _ffn(u2, x1, wg, wu, wd, g, bm, bf):
    m = x1.shape[0]
    return pl.pallas_call(
        _ffn_kernel,
        grid=(m // bm, D_FF // bf),
        in_specs=[pl.BlockSpec((bm, D_MODEL), lambda i, f: (i, 0)),
                  pl.BlockSpec((bm, D_MODEL), lambda i, f: (i, 0)),
                  pl.BlockSpec((D_MODEL, bf), lambda i, f: (0, f)),
                  pl.BlockSpec((D_MODEL, bf), lambda i, f: (0, f)),
                  pl.BlockSpec((bf, D_MODEL), lambda i, f: (f, 0)),
                  pl.BlockSpec((1, D_MODEL), lambda i, f: (0, 0))],
        out_specs=pl.BlockSpec((bm, D_MODEL), lambda i, f: (i, 0)),
        out_shape=jax.ShapeDtypeStruct((m, D_MODEL), F32),
        scratch_shapes=[pltpu.VMEM((bm, D_MODEL), F32)],
        compiler_params=_params(("parallel", "arbitrary")),
        name="ffn",
    )(u2, x1, wg, wu, wd, g)


def _layer(x, conv0, h0, s0r, s0i, w, *, bt_in, bt_lru, bt_s5, bt_merge, bm_ffn, bf, first_pos_is_zero):
    nb, seq_len = (x.shape[0], 1) if x.ndim == 2 else x.shape[:2]
    seq = dict(nb=nb, seq_len=seq_len)
    zl, zr = _inproj(x, w['norm_mix_g'], w['w_in'], bt=bt_in, **seq)
    ya, conv_n, h_n = _lru(zl, conv0, h0, w['cw'], w['cb'], w['wax'], w['bax'], w['lam'], w['proj'],
                           bt=bt_lru, first_pos_is_zero=first_pos_is_zero, **seq)
    yb, sr, si = _s5(zr, s0r, s0i, w['ca_re'], w['ca_im'], w['wb'], w['wc'], w['d'], w['wv'], w['wg'],
                     bt=bt_s5, **seq)
    x1, u2 = _merge(ya, yb, zr, x, w['w_out'], w['norm_ffn_g'], bt=bt_merge, **seq)
    m = nb * seq_len
    y = _ffn(u2.reshape(m, D_MODEL), x1.reshape(m, D_MODEL), w['ffn_wg'], w['ffn_wu'], w['ffn_wd'],
             w['norm_final_g'], bm_ffn, bf)
    return y, conv_n, h_n, sr, si


def kernel(x_prompt, x_sample, state_lru_conv, state_lru_h, state_s5_re, state_s5_im, norm_mix_g, w_in, lru_conv_w, lru_conv_b, lru_wa, lru_ba, lru_wx, lru_bx, lru_lambda, lru_proj, s5_lambda_re, s5_lambda_im, s5_log_dt, s5_b_re, s5_b_im, s5_c_re, s5_c_im, s5_d, s5_glu_wv, s5_glu_wg, w_out, norm_ffn_g, ffn_w_gate, ffn_w_up, ffn_w_down, norm_final_g):
    batch, seq, _ = x_prompt.shape
    dec_batch = x_sample.shape[0]
    l = 0
    ca_re, ca_im, wb, wc = _s5_prepare(s5_lambda_re[l], s5_lambda_im[l], s5_log_dt[l],
                                       s5_b_re[l], s5_b_im[l], s5_c_re[l], s5_c_im[l])
    w = dict(
        norm_mix_g=norm_mix_g[l][None, :], w_in=w_in[l].astype(BF16),
        cw=lru_conv_w[l], cb=lru_conv_b[l][None, :],
        wax=jnp.concatenate([lru_wa[l], lru_wx[l]], axis=-1).astype(BF16),
        bax=jnp.concatenate([lru_ba[l], lru_bx[l]], axis=-1)[:, None, :],
        lam=lru_lambda[l][None, :], proj=lru_proj[l].astype(BF16),
        ca_re=ca_re, ca_im=ca_im, wb=wb, wc=wc, d=s5_d[l].reshape(1, D_S5),
        wv=s5_glu_wv[l].astype(BF16), wg=s5_glu_wg[l].astype(BF16),
        w_out=w_out[l].astype(BF16), norm_ffn_g=norm_ffn_g[l][None, :],
        ffn_wg=ffn_w_gate[l].astype(BF16), ffn_wu=ffn_w_up[l].astype(BF16),
        ffn_wd=ffn_w_down[l].astype(BF16), norm_final_g=norm_final_g[None, :],
    )
    zeros = lambda *s: jnp.zeros(s, F32)
    yp, conv_p, h_p, sr_p, si_p = _layer(
        x_prompt, zeros(batch, CONV_W - 1, D_LRU), zeros(batch, D_LRU),
        zeros(batch, S5_N), zeros(batch, S5_N), w,
        bt_in=256, bt_lru=128, bt_s5=64, bt_merge=64, bm_ffn=512, bf=512, first_pos_is_zero=True)
    ys, conv_s, h_s, sr_s, si_s = _layer(
        x_sample.reshape(dec_batch, D_MODEL), state_lru_conv[l], state_lru_h[l],
        state_s5_re[l].reshape(dec_batch, S5_N), state_s5_im[l].reshape(dec_batch, S5_N), w,
        bt_in=1, bt_lru=1, bt_s5=1, bt_merge=1, bm_ffn=dec_batch, bf=512, first_pos_is_zero=False)
    st = lambda a, n: a.reshape(1, n, S5_G, S5_P)
    return (yp.reshape(batch, seq, D_MODEL), ys.reshape(dec_batch, 1, D_MODEL),
            conv_p[None], h_p[None], st(sr_p, batch), st(si_p, batch),
            conv_s[None], h_s[None], st(sr_s, dec_batch), st(si_s, dec_batch))
```

```python
import functools
import math

import jax
import jax.numpy as jnp
from jax import lax
from jax.experimental import pallas as pl
from jax.experimental.pallas import tpu as pltpu

F32 = jnp.float32
BF16 = jnp.bfloat16

D_MODEL = 2048
D_LRU = D_MODEL
H_LRU = 8
LRU_BW = D_LRU // H_LRU
LRU_C = 8.0
CONV_W = 4
D_S5 = D_MODEL // 2
S5_H = 16
S5_G = D_S5 // S5_H
S5_P = 64
S5_N = S5_G * S5_P
D_FF = 5632
IN_WIDTH = D_LRU + D_S5 + 2 * D_MODEL
EPS = 1e-6

S5_SLAB_G = 8
S5_SLABS = S5_G // S5_SLAB_G
S5_SLAB_IN = S5_SLAB_G * S5_H
S5_SLAB_ST = S5_SLAB_G * S5_P

LANES = 128
SUBLANES = 8
HALF = SUBLANES // 2
VMEM_LIMIT_BYTES = 56 * 1024 * 1024


def _params(sem):
    return pltpu.CompilerParams(dimension_semantics=sem, vmem_limit_bytes=VMEM_LIMIT_BYTES)


def _const_spec(shape):
    nd = len(shape)
    return pl.BlockSpec(shape, lambda *_: (0,) * nd, pipeline_mode=pl.Buffered(1))


def _rmsnorm(x, g):
    ms = jnp.mean(x * x, axis=-1, keepdims=True)
    return x * lax.rsqrt(ms + EPS) * g


def _lo_rows(shape):
    return lax.broadcasted_iota(jnp.int32, shape, 0) % SUBLANES < HALF


def _s5prep_kernel(lre_ref, lim_ref, ldt_ref, br_ref, bi_ref,
                   bbr_ref, bbi_ref, abbr_ref, abbi_ref, abr_ref, abi_ref, a2r_ref, a2i_ref):
    lre = lre_ref[...]
    lim = lim_ref[...]
    dt = jnp.exp(ldt_ref[...])
    mag = jnp.exp(lre * dt)
    ab_re = mag * jnp.cos(lim * dt)
    ab_im = mag * jnp.sin(lim * dt)
    e_re = ab_re - 1.0
    e_im = ab_im
    den = lre * lre + lim * lim
    co_re = (e_re * lre + e_im * lim) / den
    co_im = (e_im * lre - e_re * lim) / den
    br = br_ref[...]
    bi = bi_ref[...]
    bb_re = co_re * br - co_im * bi
    bb_im = co_re * bi + co_im * br
    bbr_ref[...] = bb_re
    bbi_ref[...] = bb_im
    abbr_ref[...] = ab_re * bb_re - ab_im * bb_im
    abbi_ref[...] = ab_re * bb_im + ab_im * bb_re
    abr_ref[...] = ab_re
    abi_ref[...] = ab_im
    a2r_ref[...] = ab_re * ab_re - ab_im * ab_im
    a2i_ref[...] = 2.0 * (ab_re * ab_im)


def _s5_prepare(lam_re, lam_im, log_dt, b_re, b_im, c_re, c_im):
    full = (S5_G, S5_P, S5_H)
    flat = (S5_G * S5_P * S5_H // LANES, LANES)
    bc = lambda a: jnp.broadcast_to(a, full).reshape(flat)
    args = (bc(lam_re[:, :, None]), bc(lam_im[:, :, None]), bc(log_dt[:, None, None]),
            b_re.reshape(flat), b_im.reshape(flat))
    sds = jax.ShapeDtypeStruct(flat, F32)
    bbr, bbi, abbr, abbi, abr, abi, a2r, a2i = pl.pallas_call(
        _s5prep_kernel, out_shape=(sds,) * 8, name="s5prep")(*args)
    per_state = lambda a: a.reshape(full)[:, :, 0].reshape(1, S5_N)
    coeff = lambda a1, a2: jnp.concatenate(
        [jnp.broadcast_to(per_state(a1), (HALF, S5_N)), jnp.broadcast_to(per_state(a2), (HALF, S5_N))], 0)
    ca_re, ca_im = coeff(abr, a2r), coeff(abi, a2i)
    eye = jnp.eye(S5_SLAB_G, dtype=F32)

    def slab_b(re, im):
        b = jnp.stack([re.reshape(full), im.reshape(full)], 0)
        b = b.reshape(2, S5_SLABS, S5_SLAB_G, S5_P, S5_H)
        return jnp.einsum('cjgph,gk->jghckp', b, eye).reshape(S5_SLABS, S5_SLAB_IN, 2 * S5_SLAB_ST)

    wb = jnp.concatenate([slab_b(bbr, bbi), slab_b(abbr, abbi)], axis=1)
    cc = jnp.stack([c_re, -c_im], 0).reshape(2, S5_SLABS, S5_SLAB_G, S5_H, S5_P)
    wc = jnp.einsum('cjghp,gk->jcgpkh', cc, eye).reshape(S5_SLABS, 2 * S5_SLAB_ST, S5_SLAB_IN)
    return ca_re, ca_im, wb.astype(BF16), wc.astype(BF16)


PERM_T = 64


def _perm_matrix(nb, to_time_major):
    n = nb * PERM_T
    tm = jnp.arange(n)
    sm = (tm % nb) * PERM_T + tm // nb
    p = jax.nn.one_hot(sm, n, dtype=BF16)
    return p if to_time_major else p.T


def _row_tile_spec(nb, bt, width):
    if bt == 1:
        return pl.BlockSpec((nb, width), lambda *g: (0, 0))
    return pl.BlockSpec((nb, bt, width), lambda *g: (0, g[0], 0))


N_LRU_BLOCKS = 2
INPROJ_BN = 1024
ZR_WIDTH = IN_WIDTH - D_LRU


def _zr_block(j):
    return jnp.where(j <= N_LRU_BLOCKS, (ZR_WIDTH - D_S5) // INPROJ_BN, j - N_LRU_BLOCKS - 1)


def _inproj_kernel(x_ref, g_ref, perm_ref, w_ref, zl_ref, zr_ref, u_scr, *, nb, bt):
    j = pl.program_id(1)

    @pl.when(j == 0)
    def _():
        if bt == 1:
            u_scr[...] = _rmsnorm(x_ref[...], g_ref[...]).astype(BF16)
        else:
            for s in range(bt // PERM_T):
                ts = slice(s * PERM_T, (s + 1) * PERM_T)
                xin = jnp.concatenate([x_ref[b, ts, :] for b in range(nb)], axis=0)
                u = _rmsnorm(xin, g_ref[...]).astype(BF16)
                u_scr[s * nb * PERM_T:(s + 1) * nb * PERM_T, :] = jnp.dot(
                    perm_ref[...], u, preferred_element_type=F32).astype(BF16)

    acc = jnp.dot(u_scr[...], w_ref[...], preferred_element_type=F32)

    @pl.when(j < N_LRU_BLOCKS)
    def _():
        zl_ref[...] = acc

    @pl.when(j >= N_LRU_BLOCKS)
    def _():
        zr_ref[...] = acc.astype(BF16)


def _inproj(x, g, w, *, nb, seq_len, bt):
    assert bt == 1 or bt % PERM_T == 0
    m, bm, bn = nb * seq_len, nb * bt, INPROJ_BN
    perm = _perm_matrix(nb, True) if bt > 1 else jnp.zeros((SUBLANES, LANES), BF16)
    return pl.pallas_call(
        functools.partial(_inproj_kernel, nb=nb, bt=bt),
        grid=(m // bm, IN_WIDTH // bn),
        in_specs=[_row_tile_spec(nb, bt, D_MODEL),
                  pl.BlockSpec((1, D_MODEL), lambda i, j: (0, 0)),
                  _const_spec(perm.shape),
                  pl.BlockSpec((D_MODEL, bn), lambda i, j: (0, j))],
        out_specs=(pl.BlockSpec((bm, bn), lambda i, j: (i, jnp.minimum(j, N_LRU_BLOCKS - 1))),
                   pl.BlockSpec((bm, bn), lambda i, j: (i, _zr_block(j)))),
        out_shape=(jax.ShapeDtypeStruct((m, D_LRU), F32), jax.ShapeDtypeStruct((m, ZR_WIDTH), BF16)),
        scratch_shapes=[pltpu.VMEM((bm, D_MODEL), BF16)],
        compiler_params=_params(("parallel", "arbitrary")),
        name="inproj",
    )(x, g, perm, w)


def _lru_kernel(x_ref, p0_ref, p1_ref, p2_ref, h0_ref, cw_ref, cb_ref, wax_ref, bax_ref, lam_ref,
                proj_ref, ya_ref, n0_ref, n1_ref, n2_ref, ht_ref, xe_scr, a_scr, b_scr, hc_scr, hq_scr,
                *, nb, bt, n_tiles, first_pos_is_zero):
    c = pl.program_id(0)
    rows = nb * bt
    pad = 2 * SUBLANES
    pipelined = bt > 1

    @pl.when(c == 0)
    def _():
        n0_ref[...] = p0_ref[...]
        n1_ref[...] = p1_ref[...]
        n2_ref[...] = p2_ref[...]
        ht_ref[...] = h0_ref[...]
        if pipelined:
            xe_scr[:pad - 3 * nb, :] = jnp.zeros((pad - 3 * nb, D_LRU), F32)
            hq_scr[...] = jnp.zeros(hq_scr.shape, BF16)

    nlam = -lam_ref[...]
    sp = jnp.maximum(nlam, 0.0) + jnp.log1p(jnp.exp(-jnp.abs(nlam)))

    if bt > 1:
        xe_scr[pad - 3 * nb:pad - 2 * nb, :] = n0_ref[...]
        xe_scr[pad - 2 * nb:pad - nb, :] = n1_ref[...]
        xe_scr[pad - nb:pad, :] = n2_ref[...]
        xe_scr[pad:, :] = x_ref[...]
        is_first = (lax.broadcasted_iota(jnp.int32, (rows, LRU_BW), 0) < nb) & (c == 0)

    for hb in range(H_LRU):
        cs = slice(hb * LRU_BW, (hb + 1) * LRU_BW)
        if pipelined:
            ya = jnp.dot(hq_scr[...], proj_ref[:, cs], preferred_element_type=F32)
            ya_ref[:, cs] = ya.astype(BF16)
        if bt > 1:
            xs = pltpu.roll(xe_scr[:, cs], nb, axis=0)
            xc = (cb_ref[:, cs] + xs[pad - 2 * nb:pad + rows - 2 * nb] * cw_ref[0:1, cs]
                  + xe_scr[pad - 2 * nb:pad + rows - 2 * nb, cs] * cw_ref[1:2, cs]
                  + xs[pad:] * cw_ref[2:3, cs]
                  + xe_scr[pad:, cs] * cw_ref[3:4, cs])
        else:
            xc = (cb_ref[:, cs] + n0_ref[:, cs] * cw_ref[0:1, cs] + n1_ref[:, cs] * cw_ref[1:2, cs]
                  + n2_ref[:, cs] * cw_ref[2:3, cs] + x_ref[:, cs] * cw_ref[3:4, cs])
        gates = jnp.dot(xc.astype(BF16), wax_ref[hb], preferred_element_type=F32) + bax_ref[hb]
        r = jax.nn.sigmoid(gates[:, :LRU_BW])
        i = jax.nn.sigmoid(gates[:, LRU_BW:])
        log_a = (-LRU_C) * r * sp[:, cs]
        a = jnp.exp(log_a)
        mult = jnp.sqrt(jnp.tanh(-log_a) * (1.0 + a * a))
        if first_pos_is_zero:
            mult = jnp.where(is_first, 1.0, mult)
        a_scr[:, cs] = a
        b_scr[:, cs] = mult * (i * xc)

    if not pipelined:
        n0_ref[...] = n1_ref[...]
        n1_ref[...] = n2_ref[...]
        n2_ref[...] = x_ref[...]
        h = a_scr[...] * ht_ref[...] + b_scr[...]
        ht_ref[...] = h
        ya = jnp.dot(h.astype(BF16), proj_ref[...], preferred_element_type=F32)
        ya_ref[...] = ya.astype(BF16)
        return

    @pl.when(c < n_tiles)
    def _():
        n0_ref[...] = x_ref[rows - 3 * nb:rows - 2 * nb, :]
        n1_ref[...] = x_ref[rows - 2 * nb:rows - nb, :]
        n2_ref[...] = x_ref[rows - nb:, :]

        n_cb = D_LRU // LANES
        lo = _lo_rows((SUBLANES, LANES))
        hc_scr[:HALF, :] = ht_ref[...]
        hc_scr[HALF:, :] = ht_ref[...]

        def pair(k, hs):
            rs = pl.ds(pl.multiple_of(k * SUBLANES, SUBLANES), SUBLANES)
            out = []
            for cb in range(n_cb):
                cl = slice(cb * LANES, (cb + 1) * LANES)
                av, bv = a_scr[rs, cl], b_scr[rs, cl]
                first = av * hs[cb] + bv
                second = av * pltpu.roll(first, HALF, axis=0) + bv
                h = jnp.where(lo, first, second)
                b_scr[rs, cl] = h
                out.append(pltpu.roll(h, HALF, axis=0))
            return tuple(out)

        hs = tuple(hc_scr[:, cb * LANES:(cb + 1) * LANES] for cb in range(n_cb))
        hs = lax.fori_loop(0, rows // SUBLANES, pair, hs)
        for cb in range(n_cb):
            hc_scr[:, cb * LANES:(cb + 1) * LANES] = hs[cb]
        ht_ref[...] = hc_scr[:HALF, :]
        hq_scr[...] = b_scr[...].astype(BF16)


def _lru(z, conv0, h0, cw, cb, wax, bax, lam, proj, *, nb, seq_len, bt, first_pos_is_zero):
    assert bt == 1 or (nb == HALF and bt % 2 == 0 and bt >= 4)
    rows, n_tiles = nb * bt, seq_len // bt
    pipelined = bt > 1
    p0, p1, p2 = conv0[:, 0, :], conv0[:, 1, :], conv0[:, 2, :]
    st = pl.BlockSpec((nb, D_LRU), lambda c: (0, 0))
    st_sds = jax.ShapeDtypeStruct((nb, D_LRU), F32)
    x_tile = pl.BlockSpec((rows, D_LRU), lambda c: (jnp.minimum(c, n_tiles - 1), 0))
    ya_tile = pl.BlockSpec((rows, D_LRU), lambda c: (jnp.maximum(c - int(pipelined), 0), 0))
    kern = functools.partial(_lru_kernel, nb=nb, bt=bt, n_tiles=n_tiles,
                             first_pos_is_zero=first_pos_is_zero)
    small = (SUBLANES, LANES)
    ya, n0, n1, n2, ht = pl.pallas_call(
        kern,
        grid=(n_tiles + int(pipelined),),
        in_specs=[x_tile, st, st, st, st,
                  _const_spec((CONV_W, D_LRU)), _const_spec((1, D_LRU)),
                  _const_spec((H_LRU, LRU_BW, 2 * LRU_BW)), _const_spec((H_LRU, 1, 2 * LRU_BW)),
                  _const_spec((1, D_LRU)), _const_spec((D_LRU, D_MODEL))],
        out_specs=(ya_tile, st, st, st, st),
        out_shape=(jax.ShapeDtypeStruct((nb * seq_len, D_LRU), BF16), st_sds, st_sds, st_sds, st_sds),
        scratch_shapes=[pltpu.VMEM((rows + 2 * SUBLANES, D_LRU) if pipelined else small, F32),
                        pltpu.VMEM((rows, D_LRU), F32), pltpu.VMEM((rows, D_LRU), F32),
                        pltpu.VMEM((SUBLANES, D_LRU), F32),
                        pltpu.VMEM((rows, D_LRU) if pipelined else (2 * SUBLANES, LANES), BF16)],
        compiler_params=_params(("arbitrary",)),
        name="lru",
    )(z, p0, p1, p2, h0, cw, cb, wax, bax, lam, proj)
    return ya, jnp.stack([n0, n1, n2], axis=1), ht


def _gelu_tanh(x):
    c = math.sqrt(2.0 / math.pi)
    return 0.5 * x * (1.0 + jnp.tanh(c * (x + 0.044715 * (x * x * x))))


S5_SCAN_COLS = 8 * LANES


def _s5_kernel(u_ref, s0r_ref, s0i_ref, car_ref, cai_ref, wb_ref, wc_ref, d_ref, wv_ref, wg_ref,
               yb_ref, sr_ref, si_ref, xr_scr, xi_scr, pr_scr, pi_scr, y_scr, *, nb, bt):
    c = pl.program_id(0)
    rows = nb * bt

    @pl.when(c == 0)
    def _():
        sr_ref[...] = s0r_ref[...]
        si_ref[...] = s0i_ref[...]

    ub = u_ref[...]
    u = ub.astype(F32)
    if bt > 1:
        up = jnp.where(_lo_rows(u.shape), 0.0, pltpu.roll(u, HALF, axis=0)).astype(BF16)
    for j in range(S5_SLABS):
        js = slice(j * S5_SLAB_IN, (j + 1) * S5_SLAB_IN)
        if bt > 1:
            w = jnp.dot(jnp.concatenate([ub[:, js], up[:, js]], axis=1), wb_ref[j],
                        preferred_element_type=F32)
        else:
            w = jnp.dot(ub[:, js], wb_ref[j, :S5_SLAB_IN, :], preferred_element_type=F32)
        xr_scr[:, j * S5_SLAB_ST:(j + 1) * S5_SLAB_ST] = w[:, :S5_SLAB_ST]
        xi_scr[:, j * S5_SLAB_ST:(j + 1) * S5_SLAB_ST] = w[:, S5_SLAB_ST:]

    if bt == 1:
        ar, ai = car_ref[0:1, :], cai_ref[0:1, :]
        pr, pi = sr_ref[...], si_ref[...]
        nr = ar * pr - ai * pi + xr_scr[...]
        ni = ar * pi + ai * pr + xi_scr[...]
        xr_scr[...] = nr
        xi_scr[...] = ni
        sr_ref[...] = nr
        si_ref[...] = ni
    else:
        lo = _lo_rows((SUBLANES, LANES))
        pr_scr[:HALF, :] = sr_ref[...]
        pr_scr[HALF:, :] = sr_ref[...]
        pi_scr[:HALF, :] = si_ref[...]
        pi_scr[HALF:, :] = si_ref[...]
        n_cb = S5_SCAN_COLS // LANES
        for grp in range(S5_N // S5_SCAN_COLS):
            cols = [slice(grp * S5_SCAN_COLS + cb * LANES, grp * S5_SCAN_COLS + (cb + 1) * LANES)
                    for cb in range(n_cb)]

            def pair(k, carry, cols=cols):
                rs = pl.ds(pl.multiple_of(k * SUBLANES, SUBLANES), SUBLANES)
                out = []
                for cb, cl in enumerate(cols):
                    pr, pi = carry[2 * cb], carry[2 * cb + 1]
                    ar, ai = car_ref[:, cl], cai_ref[:, cl]
                    nr = ar * pr - ai * pi + xr_scr[rs, cl]
                    ni = ar * pi + ai * pr + xi_scr[rs, cl]
                    xr_scr[rs, cl] = nr
                    xi_scr[rs, cl] = ni
                    out.append(jnp.where(lo, pltpu.roll(nr, HALF, axis=0), nr))
                    out.append(jnp.where(lo, pltpu.roll(ni, HALF, axis=0), ni))
                return tuple(out)

            carry = []
            for cl in cols:
                carry += [pr_scr[:, cl], pi_scr[:, cl]]
            carry = lax.fori_loop(0, rows // SUBLANES, pair, tuple(carry))
            for cb, cl in enumerate(cols):
                pr_scr[:, cl] = carry[2 * cb]
                pi_scr[:, cl] = carry[2 * cb + 1]
        sr_ref[...] = pr_scr[:HALF, :]
        si_ref[...] = pi_scr[:HALF, :]

    for j in range(S5_SLABS):
        ss = slice(j * S5_SLAB_ST, (j + 1) * S5_SLAB_ST)
        xs = jnp.concatenate([xr_scr[:, ss].astype(BF16), xi_scr[:, ss].astype(BF16)], axis=1)
        y_scr[:, j * S5_SLAB_IN:(j + 1) * S5_SLAB_IN] = jnp.dot(
            xs, wc_ref[j], preferred_element_type=F32)
    y = y_scr[...] + d_ref[...] * u
    v = _gelu_tanh(y).astype(BF16)
    yv = jnp.dot(v, wv_ref[...], preferred_element_type=F32)
    yg = jnp.dot(v, wg_ref[...], preferred_element_type=F32)
    yb_ref[...] = (yv * jax.nn.sigmoid(yg)).astype(BF16)


def _s5(z, s0r, s0i, ca_re, ca_im, wb, wc, d, wv, wg, *, nb, seq_len, bt):
    assert bt == 1 or (nb == HALF and bt % 2 == 0)
    rows = nb * bt
    st = pl.BlockSpec((nb, S5_N), lambda c: (0, 0))
    st_sds = jax.ShapeDtypeStruct((nb, S5_N), F32)
    x_s5_block = (ZR_WIDTH - D_S5) // D_S5
    yb, sr, si = pl.pallas_call(
        functools.partial(_s5_kernel, nb=nb, bt=bt),
        grid=(seq_len // bt,),
        in_specs=[pl.BlockSpec((rows, D_S5), lambda c: (c, x_s5_block)), st, st,
                  _const_spec((SUBLANES, S5_N)), _const_spec((SUBLANES, S5_N)),
                  _const_spec((S5_SLABS, 2 * S5_SLAB_IN, 2 * S5_SLAB_ST)),
                  _const_spec((S5_SLABS, 2 * S5_SLAB_ST, S5_SLAB_IN)),
                  _const_spec((1, D_S5)), _const_spec((D_S5, D_MODEL)), _const_spec((D_S5, D_MODEL))],
        out_specs=(pl.BlockSpec((rows, D_MODEL), lambda c: (c, 0)), st, st),
        out_shape=(jax.ShapeDtypeStruct((nb * seq_len, D_MODEL), BF16), st_sds, st_sds),
        scratch_shapes=[pltpu.VMEM((rows, S5_N), F32), pltpu.VMEM((rows, S5_N), F32),
                        pltpu.VMEM((SUBLANES, S5_N), F32), pltpu.VMEM((SUBLANES, S5_N), F32),
                        pltpu.VMEM((rows, D_S5), F32)],
        compiler_params=_params(("arbitrary",)),
        name="s5",
    )(z, s0r, s0i, ca_re, ca_im, wb, wc, d, wv, wg)
    return yb, sr, si


def _merge_kernel(ya_ref, yb_ref, ga_ref, gb_ref, x_ref, perm_ref, w_ref, g_ref, x1_ref, u2_ref,
                  *, nb, bt):
    def gated(rs):
        f32 = lambda ref: ref[rs, :].astype(F32)
        merged = jax.nn.sigmoid(f32(ga_ref)) * f32(ya_ref) + jax.nn.sigmoid(f32(gb_ref)) * f32(yb_ref)
        return merged.astype(BF16)

    if bt == 1:
        x1 = x_ref[...] + jnp.dot(gated(slice(None)), w_ref[...], preferred_element_type=F32)
        x1_ref[...] = x1
        u2_ref[...] = _rmsnorm(x1, g_ref[...]).astype(BF16)
        return
    for s in range(bt // PERM_T):
        ts = slice(s * PERM_T, (s + 1) * PERM_T)
        mb = jnp.dot(perm_ref[...], gated(slice(s * nb * PERM_T, (s + 1) * nb * PERM_T)),
                     preferred_element_type=F32).astype(BF16)
        delta = jnp.dot(mb, w_ref[...], preferred_element_type=F32)
        for b in range(nb):
            x1 = x_ref[b, ts, :] + delta[b * PERM_T:(b + 1) * PERM_T]
            x1_ref[b, ts, :] = x1
            u2_ref[b, ts, :] = _rmsnorm(x1, g_ref[...]).astype(BF16)


def _merge(ya, yb, zr, x, w_out, g, *, nb, seq_len, bt):
    assert bt == 1 or bt % PERM_T == 0
    bm = nb * bt
    row = lambda col: pl.BlockSpec((bm, D_MODEL), lambda i: (i, col))
    perm = _perm_matrix(nb, False) if bt > 1 else jnp.zeros((SUBLANES, LANES), BF16)
    tile = _row_tile_spec(nb, bt, D_MODEL)
    return pl.pallas_call(
        functools.partial(_merge_kernel, nb=nb, bt=bt),
        grid=(seq_len // bt,),
        in_specs=[row(0), row(0), row(0), row(1), tile, _const_spec(perm.shape),
                  _const_spec((D_MODEL, D_MODEL)), _const_spec((1, D_MODEL))],
        out_specs=(tile, tile),
        out_shape=(jax.ShapeDtypeStruct(x.shape, F32), jax.ShapeDtypeStruct(x.shape, BF16)),
        compiler_params=_params(("parallel",)),
        name="merge",
    )(ya, yb, zr, zr, x, perm, w_out, g)


def _ffn_kernel(u_ref, x1_ref, wg_ref, wu_ref, wd_ref, g_ref, y_ref, acc_scr):
    f = pl.program_id(1)

    @pl.when(f == 0)
    def _():
        acc_scr[...] = x1_ref[...]

    u = u_ref[...]
    hg = jnp.dot(u, wg_ref[...], preferred_element_type=F32)
    hu = jnp.dot(u, wu_ref[...], preferred_element_type=F32)
    hid = (hg * jax.nn.sigmoid(hg) * hu).astype(BF16)
    acc_scr[...] += jnp.dot(hid, wd_ref[...], preferred_element_type=F32)

    @pl.when(f == pl.num_programs(1) - 1)
    def _():
        y_ref[...] = _rmsnorm(acc_scr[...], g_ref[...])


def _ffn(u2, x1, wg, wu, wd, g, bm, bf):
    m = x1.shape[0]
    return pl.pallas_call(
        _ffn_kernel,
        grid=(m // bm, D_FF // bf),
        in_specs=[pl.BlockSpec((bm, D_MODEL), lambda i, f: (i, 0)),
                  pl.BlockSpec((bm, D_MODEL), lambda i, f: (i, 0)),
                  pl.BlockSpec((D_MODEL, bf), lambda i, f: (0, f)),
                  pl.BlockSpec((D_MODEL, bf), lambda i, f: (0, f)),
                  pl.BlockSpec((bf, D_MODEL), lambda i, f: (f, 0)),
                  pl.BlockSpec((1, D_MODEL), lambda i, f: (0, 0))],
        out_specs=pl.BlockSpec((bm, D_MODEL), lambda i, f: (i, 0)),
        out_shape=jax.ShapeDtypeStruct((m, D_MODEL), F32),
        scratch_shapes=[pltpu.VMEM((bm, D_MODEL), F32)],
        compiler_params=_params(("parallel", "arbitrary")),
        name="ffn",
    )(u2, x1, wg, wu, wd, g)


def _layer(x, conv0, h0, s0r, s0i, w, *, bt_in, bt_lru, bt_s5, bt_merge, bm_ffn, bf, first_pos_is_zero):
    nb, seq_len = (x.shape[0], 1) if x.ndim == 2 else x.shape[:2]
    seq = dict(nb=nb, seq_len=seq_len)
    zl, zr = _inproj(x, w['norm_mix_g'], w['w_in'], bt=bt_in, **seq)
    ya, conv_n, h_n = _lru(zl, conv0, h0, w['cw'], w['cb'], w['wax'], w['bax'], w['lam'], w['proj'],
                           bt=bt_lru, first_pos_is_zero=first_pos_is_zero, **seq)
    yb, sr, si = _s5(zr, s0r, s0i, w['ca_re'], w['ca_im'], w['wb'], w['wc'], w['d'], w['wv'], w['wg'],
                     bt=bt_s5, **seq)
    x1, u2 = _merge(ya, yb, zr, x, w['w_out'], w['norm_ffn_g'], bt=bt_merge, **seq)
    m = nb * seq_len
    y = _ffn(u2.reshape(m, D_MODEL), x1.reshape(m, D_MODEL), w['ffn_wg'], w['ffn_wu'], w['ffn_wd'],
             w['norm_final_g'], bm_ffn, bf)
    return y, conv_n, h_n, sr, si


def kernel(x_prompt, x_sample, state_lru_conv, state_lru_h, state_s5_re, state_s5_im, norm_mix_g, w_in, lru_conv_w, lru_conv_b, lru_wa, lru_ba, lru_wx, lru_bx, lru_lambda, lru_proj, s5_lambda_re, s5_lambda_im, s5_log_dt, s5_b_re, s5_b_im, s5_c_re, s5_c_im, s5_d, s5_glu_wv, s5_glu_wg, w_out, norm_ffn_g, ffn_w_gate, ffn_w_up, ffn_w_down, norm_final_g):
    batch, seq, _ = x_prompt.shape
    dec_batch = x_sample.shape[0]
    l = 0
    ca_re, ca_im, wb, wc = _s5_prepare(s5_lambda_re[l], s5_lambda_im[l], s5_log_dt[l],
                                       s5_b_re[l], s5_b_im[l], s5_c_re[l], s5_c_im[l])
    w = dict(
        norm_mix_g=norm_mix_g[l][None, :], w_in=w_in[l].astype(BF16),
        cw=lru_conv_w[l], cb=lru_conv_b[l][None, :],
        wax=jnp.concatenate([lru_wa[l], lru_wx[l]], axis=-1).astype(BF16),
        bax=jnp.concatenate([lru_ba[l], lru_bx[l]], axis=-1)[:, None, :],
        lam=lru_lambda[l][None, :], proj=lru_proj[l].astype(BF16),
        ca_re=ca_re, ca_im=ca_im, wb=wb, wc=wc, d=s5_d[l].reshape(1, D_S5),
        wv=s5_glu_wv[l].astype(BF16), wg=s5_glu_wg[l].astype(BF16),
        w_out=w_out[l].astype(BF16), norm_ffn_g=norm_ffn_g[l][None, :],
        ffn_wg=ffn_w_gate[l].astype(BF16), ffn_wu=ffn_w_up[l].astype(BF16),
        ffn_wd=ffn_w_down[l].astype(BF16),
        norm_final_g=norm_final_g[None, :],
    )
    zeros = lambda *s: jnp.zeros(s, F32)
    yp, conv_p, h_p, sr_p, si_p = _layer(
        x_prompt, zeros(batch, CONV_W - 1, D_LRU), zeros(batch, D_LRU),
        zeros(batch, S5_N), zeros(batch, S5_N), w,
        bt_in=256, bt_lru=128, bt_s5=64, bt_merge=128, bm_ffn=512, bf=512, first_pos_is_zero=True)
    ys, conv_s, h_s, sr_s, si_s = _layer(
        x_sample.reshape(dec_batch, D_MODEL), state_lru_conv[l], state_lru_h[l],
        state_s5_re[l].reshape(dec_batch, S5_N), state_s5_im[l].reshape(dec_batch, S5_N), w,
        bt_in=1, bt_lru=1, bt_s5=1, bt_merge=1, bm_ffn=dec_batch, bf=512, first_pos_is_zero=False)
    st = lambda a, n: a.reshape(1, n, S5_G, S5_P)
    return (yp.reshape(batch, seq, D_MODEL), ys.reshape(dec_batch, 1, D_MODEL),
            conv_p[None], h_p[None], st(sr_p, batch), st(si_p, batch),
            conv_s[None], h_s[None], st(sr_s, dec_batch), st(si_s, dec_batch))
```

```python
import functools
import math

import jax
import jax.numpy as jnp
from jax import lax
from jax.experimental import pallas as pl
from jax.experimental.pallas import tpu as pltpu

F32 = jnp.float32
BF16 = jnp.bfloat16

D_MODEL = 2048
D_LRU = D_MODEL
H_LRU = 8
LRU_BW = D_LRU // H_LRU
LRU_C = 8.0
CONV_W = 4
D_S5 = D_MODEL // 2
S5_H = 16
S5_G = D_S5 // S5_H
S5_P = 64
S5_N = S5_G * S5_P
D_FF = 5632
IN_WIDTH = D_LRU + D_S5 + 2 * D_MODEL
EPS = 1e-6

S5_SLAB_G = 8
S5_SLABS = S5_G // S5_SLAB_G
S5_SLAB_IN = S5_SLAB_G * S5_H
S5_SLAB_ST = S5_SLAB_G * S5_P

LANES = 128
SUBLANES = 8
HALF = SUBLANES // 2
VMEM_LIMIT_BYTES = 56 * 1024 * 1024


def _params(sem):
    return pltpu.CompilerParams(dimension_semantics=sem, vmem_limit_bytes=VMEM_LIMIT_BYTES)


def _const_spec(shape):
    nd = len(shape)
    return pl.BlockSpec(shape, lambda *_: (0,) * nd, pipeline_mode=pl.Buffered(1))


def _rmsnorm(x, g):
    ms = jnp.mean(x * x, axis=-1, keepdims=True)
    return x * lax.rsqrt(ms + EPS) * g


def _lo_rows(shape):
    return lax.broadcasted_iota(jnp.int32, shape, 0) % SUBLANES < HALF


def _s5prep_kernel(lre_ref, lim_ref, ldt_ref, br_ref, bi_ref,
                   bbr_ref, bbi_ref, abbr_ref, abbi_ref, abr_ref, abi_ref, a2r_ref, a2i_ref):
    lre = lre_ref[...]
    lim = lim_ref[...]
    dt = jnp.exp(ldt_ref[...])
    mag = jnp.exp(lre * dt)
    ab_re = mag * jnp.cos(lim * dt)
    ab_im = mag * jnp.sin(lim * dt)
    e_re = ab_re - 1.0
    e_im = ab_im
    den = lre * lre + lim * lim
    co_re = (e_re * lre + e_im * lim) / den
    co_im = (e_im * lre - e_re * lim) / den
    br = br_ref[...]
    bi = bi_ref[...]
    bb_re = co_re * br - co_im * bi
    bb_im = co_re * bi + co_im * br
    bbr_ref[...] = bb_re
    bbi_ref[...] = bb_im
    abbr_ref[...] = ab_re * bb_re - ab_im * bb_im
    abbi_ref[...] = ab_re * bb_im + ab_im * bb_re
    abr_ref[...] = ab_re
    abi_ref[...] = ab_im
    a2r_ref[...] = ab_re * ab_re - ab_im * ab_im
    a2i_ref[...] = 2.0 * (ab_re * ab_im)


def _s5_prepare(lam_re, lam_im, log_dt, b_re, b_im, c_re, c_im):
    full = (S5_G, S5_P, S5_H)
    flat = (S5_G * S5_P * S5_H // LANES, LANES)
    bc = lambda a: jnp.broadcast_to(a, full).reshape(flat)
    args = (bc(lam_re[:, :, None]), bc(lam_im[:, :, None]), bc(log_dt[:, None, None]),
            b_re.reshape(flat), b_im.reshape(flat))
    sds = jax.ShapeDtypeStruct(flat, F32)
    bbr, bbi, abbr, abbi, abr, abi, a2r, a2i = pl.pallas_call(
        _s5prep_kernel, out_shape=(sds,) * 8, name="s5prep")(*args)
    per_state = lambda a: a.reshape(full)[:, :, 0].reshape(1, S5_N)
    coeff = lambda a1, a2: jnp.concatenate(
        [jnp.broadcast_to(per_state(a1), (HALF, S5_N)), jnp.broadcast_to(per_state(a2), (HALF, S5_N))], 0)
    ca_re, ca_im = coeff(abr, a2r), coeff(abi, a2i)
    eye = jnp.eye(S5_SLAB_G, dtype=F32)

    def slab_b(re, im):
        b = jnp.stack([re.reshape(full), im.reshape(full)], 0)
        b = b.reshape(2, S5_SLABS, S5_SLAB_G, S5_P, S5_H)
        return jnp.einsum('cjgph,gk->jghckp', b, eye).reshape(S5_SLABS, S5_SLAB_IN, 2 * S5_SLAB_ST)

    wb = jnp.concatenate([slab_b(bbr, bbi), slab_b(abbr, abbi)], axis=1)
    cc = jnp.stack([c_re, -c_im], 0).reshape(2, S5_SLABS, S5_SLAB_G, S5_H, S5_P)
    wc = jnp.einsum('cjghp,gk->jcgpkh', cc, eye).reshape(S5_SLABS, 2 * S5_SLAB_ST, S5_SLAB_IN)
    return ca_re, ca_im, wb.astype(BF16), wc.astype(BF16)


PERM_T = 64


def _perm_matrix(nb, to_time_major):
    n = nb * PERM_T
    tm = jnp.arange(n)
    sm = (tm % nb) * PERM_T + tm // nb
    p = jax.nn.one_hot(sm, n, dtype=BF16)
    return p if to_time_major else p.T


def _row_tile_spec(nb, bt, width):
    if bt == 1:
        return pl.BlockSpec((nb, width), lambda *g: (0, 0))
    return pl.BlockSpec((nb, bt, width), lambda *g: (0, g[0], 0))


N_LRU_BLOCKS = 2
INPROJ_BN = 1024
ZR_WIDTH = IN_WIDTH - D_LRU


def _zr_block(j):
    return jnp.where(j <= N_LRU_BLOCKS, (ZR_WIDTH - D_S5) // INPROJ_BN, j - N_LRU_BLOCKS - 1)


def _inproj_kernel(x_ref, g_ref, perm_ref, w_ref, zl_ref, zr_ref, u_scr, *, nb, bt):
    j = pl.program_id(1)

    @pl.when(j == 0)
    def _():
        if bt == 1:
            u_scr[...] = _rmsnorm(x_ref[...], g_ref[...]).astype(BF16)
        else:
            for s in range(bt // PERM_T):
                ts = slice(s * PERM_T, (s + 1) * PERM_T)
                xin = jnp.concatenate([x_ref[b, ts, :] for b in range(nb)], axis=0)
                u = _rmsnorm(xin, g_ref[...]).astype(BF16)
                u_scr[s * nb * PERM_T:(s + 1) * nb * PERM_T, :] = jnp.dot(
                    perm_ref[...], u, preferred_element_type=F32).astype(BF16)

    acc = jnp.dot(u_scr[...], w_ref[...], preferred_element_type=F32)

    @pl.when(j < N_LRU_BLOCKS)
    def _():
        zl_ref[...] = acc

    @pl.when(j >= N_LRU_BLOCKS)
    def _():
        zr_ref[...] = acc.astype(BF16)


def _inproj(x, g, w, *, nb, seq_len, bt):
    assert bt == 1 or bt % PERM_T == 0
    m, bm, bn = nb * seq_len, nb * bt, INPROJ_BN
    perm = _perm_matrix(nb, True) if bt > 1 else jnp.zeros((SUBLANES, LANES), BF16)
    return pl.pallas_call(
        functools.partial(_inproj_kernel, nb=nb, bt=bt),
        grid=(m // bm, IN_WIDTH // bn),
        in_specs=[_row_tile_spec(nb, bt, D_MODEL),
                  pl.BlockSpec((1, D_MODEL), lambda i, j: (0, 0)),
                  _const_spec(perm.shape),
                  pl.BlockSpec((D_MODEL, bn), lambda i, j: (0, j))],
        out_specs=(pl.BlockSpec((bm, bn), lambda i, j: (i, jnp.minimum(j, N_LRU_BLOCKS - 1))),
                   pl.BlockSpec((bm, bn), lambda i, j: (i, _zr_block(j)))),
        out_shape=(jax.ShapeDtypeStruct((m, D_LRU), F32), jax.ShapeDtypeStruct((m, ZR_WIDTH), BF16)),
        scratch_shapes=[pltpu.VMEM((bm, D_MODEL), BF16)],
        compiler_params=_params(("parallel", "arbitrary")),
        name="inproj",
    )(x, g, perm, w)


def _lru_kernel(x_ref, p0_ref, p1_ref, p2_ref, h0_ref, cw_ref, cb_ref, wax_ref, bax_ref, lam_ref,
                proj_ref, ya_ref, n0_ref, n1_ref, n2_ref, ht_ref, xe_scr, a_scr, b_scr, hc_scr, hq_scr,
                *, nb, bt, n_tiles, first_pos_is_zero):
    c = pl.program_id(0)
    rows = nb * bt
    pad = 2 * SUBLANES
    pipelined = bt > 1

    @pl.when(c == 0)
    def _():
        n0_ref[...] = p0_ref[...]
        n1_ref[...] = p1_ref[...]
        n2_ref[...] = p2_ref[...]
        ht_ref[...] = h0_ref[...]
        if pipelined:
            xe_scr[:pad - 3 * nb, :] = jnp.zeros((pad - 3 * nb, D_LRU), F32)
            hq_scr[...] = jnp.zeros(hq_scr.shape, BF16)

    nlam = -lam_ref[...]
    sp = jnp.maximum(nlam, 0.0) + jnp.log1p(jnp.exp(-jnp.abs(nlam)))

    if bt > 1:
        xe_scr[pad - 3 * nb:pad - 2 * nb, :] = n0_ref[...]
        xe_scr[pad - 2 * nb:pad - nb, :] = n1_ref[...]
        xe_scr[pad - nb:pad, :] = n2_ref[...]
        xe_scr[pad:, :] = x_ref[...]
        is_first = (lax.broadcasted_iota(jnp.int32, (rows, LRU_BW), 0) < nb) & (c == 0)

    for hb in range(H_LRU):
        cs = slice(hb * LRU_BW, (hb + 1) * LRU_BW)
        if pipelined:
            ya = jnp.dot(hq_scr[...], proj_ref[:, cs], preferred_element_type=F32)
            ya_ref[:, cs] = ya.astype(BF16)
        if bt > 1:
            xs = pltpu.roll(xe_scr[:, cs], nb, axis=0)
            xc = (cb_ref[:, cs] + xs[pad - 2 * nb:pad + rows - 2 * nb] * cw_ref[0:1, cs]
                  + xe_scr[pad - 2 * nb:pad + rows - 2 * nb, cs] * cw_ref[1:2, cs]
                  + xs[pad:] * cw_ref[2:3, cs]
                  + xe_scr[pad:, cs] * cw_ref[3:4, cs])
        else:
            xc = (cb_ref[:, cs] + n0_ref[:, cs] * cw_ref[0:1, cs] + n1_ref[:, cs] * cw_ref[1:2, cs]
                  + n2_ref[:, cs] * cw_ref[2:3, cs] + x_ref[:, cs] * cw_ref[3:4, cs])
        gates = jnp.dot(xc.astype(BF16), wax_ref[hb], preferred_element_type=F32) + bax_ref[hb]
        r = jax.nn.sigmoid(gates[:, :LRU_BW])
        i = jax.nn.sigmoid(gates[:, LRU_BW:])
        log_a = (-LRU_C) * r * sp[:, cs]
        a = jnp.exp(log_a)
        mult = jnp.sqrt(jnp.tanh(-log_a) * (1.0 + a * a))
        if first_pos_is_zero:
            mult = jnp.where(is_first, 1.0, mult)
        a_scr[:, cs] = a
        b_scr[:, cs] = mult * (i * xc)

    if not pipelined:
        n0_ref[...] = n1_ref[...]
        n1_ref[...] = n2_ref[...]
        n2_ref[...] = x_ref[...]
        h = a_scr[...] * ht_ref[...] + b_scr[...]
        ht_ref[...] = h
        ya = jnp.dot(h.astype(BF16), proj_ref[...], preferred_element_type=F32)
        ya_ref[...] = ya.astype(BF16)
        return

    @pl.when(c < n_tiles)
    def _():
        n0_ref[...] = x_ref[rows - 3 * nb:rows - 2 * nb, :]
        n1_ref[...] = x_ref[rows - 2 * nb:rows - nb, :]
        n2_ref[...] = x_ref[rows - nb:, :]

        n_cb = D_LRU // LANES
        lo = _lo_rows((SUBLANES, LANES))
        hc_scr[:HALF, :] = ht_ref[...]
        hc_scr[HALF:, :] = ht_ref[...]

        def pair(k, hs):
            rs = pl.ds(pl.multiple_of(k * SUBLANES, SUBLANES), SUBLANES)
            out = []
            for cb in range(n_cb):
                cl = slice(cb * LANES, (cb + 1) * LANES)
                av, bv = a_scr[rs, cl], b_scr[rs, cl]
                first = av * hs[cb] + bv
                second = av * pltpu.roll(first, HALF, axis=0) + bv
                h = jnp.where(lo, first, second)
                b_scr[rs, cl] = h
                out.append(pltpu.roll(h, HALF, axis=0))
            return tuple(out)

        hs = tuple(hc_scr[:, cb * LANES:(cb + 1) * LANES] for cb in range(n_cb))
        hs = lax.fori_loop(0, rows // SUBLANES, pair, hs)
        for cb in range(n_cb):
            hc_scr[:, cb * LANES:(cb + 1) * LANES] = hs[cb]
        ht_ref[...] = hc_scr[:HALF, :]
        hq_scr[...] = b_scr[...].astype(BF16)


def _lru(z, conv0, h0, cw, cb, wax, bax, lam, proj, *, nb, seq_len, bt, first_pos_is_zero):
    assert bt == 1 or (nb == HALF and bt % 2 == 0 and bt >= 4)
    rows, n_tiles = nb * bt, seq_len // bt
    pipelined = bt > 1
    p0, p1, p2 = conv0[:, 0, :], conv0[:, 1, :], conv0[:, 2, :]
    st = pl.BlockSpec((nb, D_LRU), lambda c: (0, 0))
    st_sds = jax.ShapeDtypeStruct((nb, D_LRU), F32)
    x_tile = pl.BlockSpec((rows, D_LRU), lambda c: (jnp.minimum(c, n_tiles - 1), 0))
    ya_tile = pl.BlockSpec((rows, D_LRU), lambda c: (jnp.maximum(c - int(pipelined), 0), 0))
    kern = functools.partial(_lru_kernel, nb=nb, bt=bt, n_tiles=n_tiles,
                             first_pos_is_zero=first_pos_is_zero)
    small = (SUBLANES, LANES)
    ya, n0, n1, n2, ht = pl.pallas_call(
        kern,
        grid=(n_tiles + int(pipelined),),
        in_specs=[x_tile, st, st, st, st,
                  _const_spec((CONV_W, D_LRU)), _const_spec((1, D_LRU)),
                  _const_spec((H_LRU, LRU_BW, 2 * LRU_BW)), _const_spec((H_LRU, 1, 2 * LRU_BW)),
                  _const_spec((1, D_LRU)), _const_spec((D_LRU, D_MODEL))],
        out_specs=(ya_tile, st, st, st, st),
        out_shape=(jax.ShapeDtypeStruct((nb * seq_len, D_LRU), BF16), st_sds, st_sds, st_sds, st_sds),
        scratch_shapes=[pltpu.VMEM((rows + 2 * SUBLANES, D_LRU) if pipelined else small, F32),
                        pltpu.VMEM((rows, D_LRU), F32), pltpu.VMEM((rows, D_LRU), F32),
                        pltpu.VMEM((SUBLANES, D_LRU), F32),
                        pltpu.VMEM((rows, D_LRU) if pipelined else (2 * SUBLANES, LANES), BF16)],
        compiler_params=_params(("arbitrary",)),
        name="lru",
    )(z, p0, p1, p2, h0, cw, cb, wax, bax, lam, proj)
    return ya, jnp.stack([n0, n1, n2], axis=1), ht


def _gelu_tanh(x):
    c = math.sqrt(2.0 / math.pi)
    return 0.5 * x * (1.0 + jnp.tanh(c * (x + 0.044715 * (x * x * x))))


S5_GROUPS = 4
S5_GROUP_SLABS = S5_SLABS // S5_GROUPS
S5_GROUP_COLS = S5_N // S5_GROUPS
GLU_GROUP_COLS = D_MODEL // S5_GROUPS


def _s5_b_proj(ub, up, wb_ref, xr_scr, xi_scr, j):
    js = slice(j * S5_SLAB_IN, (j + 1) * S5_SLAB_IN)
    if up is None:
        w = jnp.dot(ub[:, js], wb_ref[j, :S5_SLAB_IN, :], preferred_element_type=F32)
    else:
        w = jnp.dot(jnp.concatenate([ub[:, js], up[:, js]], axis=1), wb_ref[j],
                    preferred_element_type=F32)
    xr_scr[:, j * S5_SLAB_ST:(j + 1) * S5_SLAB_ST] = w[:, :S5_SLAB_ST]
    xi_scr[:, j * S5_SLAB_ST:(j + 1) * S5_SLAB_ST] = w[:, S5_SLAB_ST:]


def _s5_c_proj(xr_scr, xi_scr, wc_ref, y_scr, j):
    ss = slice(j * S5_SLAB_ST, (j + 1) * S5_SLAB_ST)
    xs = jnp.concatenate([xr_scr[:, ss].astype(BF16), xi_scr[:, ss].astype(BF16)], axis=1)
    y_scr[:, j * S5_SLAB_IN:(j + 1) * S5_SLAB_IN] = jnp.dot(xs, wc_ref[j], preferred_element_type=F32)


def _s5_kernel(u_ref, s0r_ref, s0i_ref, car_ref, cai_ref, wb_ref, wc_ref, d_ref, wv_ref, wg_ref,
               yb_ref, sr_ref, si_ref, xr_scr, xi_scr, pr_scr, pi_scr, y_scr, v_scr,
               *, nb, bt, n_tiles):
    c = pl.program_id(0)
    rows = nb * bt
    ub = u_ref[...]
    u = ub.astype(F32)

    if bt == 1:
        for j in range(S5_SLABS):
            _s5_b_proj(ub, None, wb_ref, xr_scr, xi_scr, j)
        ar, ai = car_ref[0:1, :], cai_ref[0:1, :]
        pr, pi = s0r_ref[...], s0i_ref[...]
        nr = ar * pr - ai * pi + xr_scr[...]
        ni = ar * pi + ai * pr + xi_scr[...]
        xr_scr[...] = nr
        xi_scr[...] = ni
        sr_ref[...] = nr
        si_ref[...] = ni
        for j in range(S5_SLABS):
            _s5_c_proj(xr_scr, xi_scr, wc_ref, y_scr, j)
        v = _gelu_tanh(y_scr[...] + d_ref[...] * u).astype(BF16)
        yv = jnp.dot(v, wv_ref[...], preferred_element_type=F32)
        yg = jnp.dot(v, wg_ref[...], preferred_element_type=F32)
        yb_ref[...] = (yv * jax.nn.sigmoid(yg)).astype(BF16)
        return

    @pl.when(c == 0)
    def _():
        pr_scr[:HALF, :] = s0r_ref[...]
        pr_scr[HALF:, :] = s0r_ref[...]
        pi_scr[:HALF, :] = s0i_ref[...]
        pi_scr[HALF:, :] = s0i_ref[...]
        v_scr[...] = jnp.zeros(v_scr.shape, BF16)

    up = jnp.where(_lo_rows(u.shape), 0.0, pltpu.roll(u, HALF, axis=0)).astype(BF16)
    lo = _lo_rows((SUBLANES, LANES))
    v_prev = v_scr[...]
    for grp in range(S5_GROUPS):
        gs = slice(grp * GLU_GROUP_COLS, (grp + 1) * GLU_GROUP_COLS)
        yv = jnp.dot(v_prev, wv_ref[:, gs], preferred_element_type=F32)
        yg = jnp.dot(v_prev, wg_ref[:, gs], preferred_element_type=F32)
        yb_ref[:, gs] = (yv * jax.nn.sigmoid(yg)).astype(BF16)

        slabs = range(grp * S5_GROUP_SLABS, (grp + 1) * S5_GROUP_SLABS)
        for j in slabs:
            _s5_b_proj(ub, up, wb_ref, xr_scr, xi_scr, j)
        cols = [slice(grp * S5_GROUP_COLS + cb * LANES, grp * S5_GROUP_COLS + (cb + 1) * LANES)
                for cb in range(S5_GROUP_COLS // LANES)]
        carry = [(pr_scr[:, cl], pi_scr[:, cl]) for cl in cols]
        for k in range(rows // SUBLANES):
            rs = slice(k * SUBLANES, (k + 1) * SUBLANES)
            for cb, cl in enumerate(cols):
                pr, pi = carry[cb]
                ar, ai = car_ref[:, cl], cai_ref[:, cl]
                nr = ar * pr - ai * pi + xr_scr[rs, cl]
                ni = ar * pi + ai * pr + xi_scr[rs, cl]
                xr_scr[rs, cl] = nr
                xi_scr[rs, cl] = ni
                carry[cb] = (jnp.where(lo, pltpu.roll(nr, HALF, axis=0), nr),
                             jnp.where(lo, pltpu.roll(ni, HALF, axis=0), ni))
        for cb, cl in enumerate(cols):
            pr_scr[:, cl], pi_scr[:, cl] = carry[cb]
        for j in slabs:
            _s5_c_proj(xr_scr, xi_scr, wc_ref, y_scr, j)
    v_scr[...] = _gelu_tanh(y_scr[...] + d_ref[...] * u).astype(BF16)

    @pl.when(c == n_tiles - 1)
    def _():
        sr_ref[...] = pr_scr[:HALF, :]
        si_ref[...] = pi_scr[:HALF, :]


def _s5(z, s0r, s0i, ca_re, ca_im, wb, wc, d, wv, wg, *, nb, seq_len, bt):
    assert bt == 1 or (nb == HALF and bt % 2 == 0)
    rows, n_tiles = nb * bt, seq_len // bt
    pipelined = bt > 1
    st = pl.BlockSpec((nb, S5_N), lambda c: (0, 0))
    st_sds = jax.ShapeDtypeStruct((nb, S5_N), F32)
    x_s5_block = (ZR_WIDTH - D_S5) // D_S5
    yb, sr, si = pl.pallas_call(
        functools.partial(_s5_kernel, nb=nb, bt=bt, n_tiles=n_tiles),
        grid=(n_tiles + int(pipelined),),
        in_specs=[pl.BlockSpec((rows, D_S5), lambda c: (jnp.minimum(c, n_tiles - 1), x_s5_block)),
                  st, st, _const_spec((SUBLANES, S5_N)), _const_spec((SUBLANES, S5_N)),
                  _const_spec((S5_SLABS, 2 * S5_SLAB_IN, 2 * S5_SLAB_ST)),
                  _const_spec((S5_SLABS, 2 * S5_SLAB_ST, S5_SLAB_IN)),
                  _const_spec((1, D_S5)), _const_spec((D_S5, D_MODEL)), _const_spec((D_S5, D_MODEL))],
        out_specs=(pl.BlockSpec((rows, D_MODEL), lambda c: (jnp.maximum(c - int(pipelined), 0), 0)),
                   st, st),
        out_shape=(jax.ShapeDtypeStruct((nb * seq_len, D_MODEL), BF16), st_sds, st_sds),
        scratch_shapes=[pltpu.VMEM((rows, S5_N), F32), pltpu.VMEM((rows, S5_N), F32),
                        pltpu.VMEM((SUBLANES, S5_N), F32), pltpu.VMEM((SUBLANES, S5_N), F32),
                        pltpu.VMEM((rows, D_S5), F32), pltpu.VMEM((rows, D_S5), BF16)],
        compiler_params=_params(("arbitrary",)),
        name="s5",
    )(z, s0r, s0i, ca_re, ca_im, wb, wc, d, wv, wg)
    return yb, sr, si


def _merge_kernel(ya_ref, yb_ref, ga_ref, gb_ref, x_ref, perm_ref, w_ref, g_ref, x1_ref, u2_ref,
                  *, nb, bt):
    def gated(rs):
        f32 = lambda ref: ref[rs, :].astype(F32)
        merged = jax.nn.sigmoid(f32(ga_ref)) * f32(ya_ref) + jax.nn.sigmoid(f32(gb_ref)) * f32(yb_ref)
        return merged.astype(BF16)

    if bt == 1:
        x1 = x_ref[...] + jnp.dot(gated(slice(None)), w_ref[...], preferred_element_type=F32)
        x1_ref[...] = x1
        u2_ref[...] = _rmsnorm(x1, g_ref[...]).astype(BF16)
        return
    for s in range(bt // PERM_T):
        ts = slice(s * PERM_T, (s + 1) * PERM_T)
        mb = jnp.dot(perm_ref[...], gated(slice(s * nb * PERM_T, (s + 1) * nb * PERM_T)),
                     preferred_element_type=F32).astype(BF16)
        delta = jnp.dot(mb, w_ref[...], preferred_element_type=F32)
        for b in range(nb):
            x1 = x_ref[b, ts, :] + delta[b * PERM_T:(b + 1) * PERM_T]
            x1_ref[b, ts, :] = x1
            u2_ref[b, ts, :] = _rmsnorm(x1, g_ref[...]).astype(BF16)


def _merge(ya, yb, zr, x, w_out, g, *, nb, seq_len, bt):
    assert bt == 1 or bt % PERM_T == 0
    bm = nb * bt
    row = lambda col: pl.BlockSpec((bm, D_MODEL), lambda i: (i, col))
    perm = _perm_matrix(nb, False) if bt > 1 else jnp.zeros((SUBLANES, LANES), BF16)
    tile = _row_tile_spec(nb, bt, D_MODEL)
    return pl.pallas_call(
        functools.partial(_merge_kernel, nb=nb, bt=bt),
        grid=(seq_len // bt,),
        in_specs=[row(0), row(0), row(0), row(1), tile, _const_spec(perm.shape),
                  _const_spec((D_MODEL, D_MODEL)), _const_spec((1, D_MODEL))],
        out_specs=(tile, tile),
        out_shape=(jax.ShapeDtypeStruct(x.shape, F32), jax.ShapeDtypeStruct(x.shape, BF16)),
        compiler_params=_params(("parallel",)),
        name="merge",
    )(ya, yb, zr, zr, x, perm, w_out, g)


def _ffn_kernel(u_ref, x1_ref, wg_ref, wu_ref, wd_ref, g_ref, y_ref, acc_scr):
    f = pl.program_id(1)

    @pl.when(f == 0)
    def _():
        acc_scr[...] = x1_ref[...]

    u = u_ref[...]
    hg = jnp.dot(u, wg_ref[...], preferred_element_type=F32)
    hu = jnp.dot(u, wu_ref[...], preferred_element_type=F32)
    hid = (hg * jax.nn.sigmoid(hg) * hu).astype(BF16)
    acc_scr[...] += jnp.dot(hid, wd_ref[...], preferred_element_type=F32)

    @pl.when(f == pl.num_programs(1) - 1)
    def _():
        y_ref[...] = _rmsnorm(acc_scr[...], g_ref[...])


def _ffn(u2, x1, wg, wu, wd, g, bm, bf):
    m = x1.shape[0]
    return pl.pallas_call(
        _ffn_kernel,
        grid=(m // bm, D_FF // bf),
        in_specs=[pl.BlockSpec((bm, D_MODEL), lambda i, f: (i, 0)),
                  pl.BlockSpec((bm, D_MODEL), lambda i, f: (i, 0)),
                  pl.BlockSpec((D_MODEL, bf), lambda i, f: (0, f)),
                  pl.BlockSpec((D_MODEL, bf), lambda i, f: (0, f)),
                  pl.BlockSpec((bf, D_MODEL), lambda i, f: (f, 0)),
                  pl.BlockSpec((1, D_MODEL), lambda i, f: (0, 0))],
        out_specs=pl.BlockSpec((bm, D_MODEL), lambda i, f: (i, 0)),
        out_shape=jax.ShapeDtypeStruct((m, D_MODEL), F32),
        scratch_shapes=[pltpu.VMEM((bm, D_MODEL), F32)],
        compiler_params=_params(("parallel", "arbitrary")),
        name="ffn",
    )(u2, x1, wg, wu, wd, g)


def _layer(x, conv0, h0, s0r, s0i, w, *, bt_in, bt_lru, bt_s5, bt_merge, bm_ffn, bf, first_pos_is_zero):
    nb, seq_len = (x.shape[0], 1) if x.ndim == 2 else x.shape[:2]
    seq = dict(nb=nb, seq_len=seq_len)
    zl, zr = _inproj(x, w['norm_mix_g'], w['w_in'], bt=bt_in, **seq)
    ya, conv_n, h_n = _lru(zl, conv0, h0, w['cw'], w['cb'], w['wax'], w['bax'], w['lam'], w['proj'],
                           bt=bt_lru, first_pos_is_zero=first_pos_is_zero, **seq)
    yb, sr, si = _s5(zr, s0r, s0i, w['ca_re'], w['ca_im'], w['wb'], w['wc'], w['d'], w['wv'], w['wg'],
                     bt=bt_s5, **seq)
    x1, u2 = _merge(ya, yb, zr, x, w['w_out'], w['norm_ffn_g'], bt=bt_merge, **seq)
    m = nb * seq_len
    y = _ffn(u2.reshape(m, D_MODEL), x1.reshape(m, D_MODEL), w['ffn_wg'], w['ffn_wu'], w['ffn_wd'],
             w['norm_final_g'], bm_ffn, bf)
    return y, conv_n, h_n, sr, si


def kernel(x_prompt, x_sample, state_lru_conv, state_lru_h, state_s5_re, state_s5_im, norm_mix_g, w_in, lru_conv_w, lru_conv_b, lru_wa, lru_ba, lru_wx, lru_bx, lru_lambda, lru_proj, s5_lambda_re, s5_lambda_im, s5_log_dt, s5_b_re, s5_b_im, s5_c_re, s5_c_im, s5_d, s5_glu_wv, s5_glu_wg, w_out, norm_ffn_g, ffn_w_gate, ffn_w_up, ffn_w_down, norm_final_g):
    batch, seq, _ = x_prompt.shape
    dec_batch = x_sample.shape[0]
    l = 0
    ca_re, ca_im, wb, wc = _s5_prepare(s5_lambda_re[l], s5_lambda_im[l], s5_log_dt[l],
                                       s5_b_re[l], s5_b_im[l], s5_c_re[l], s5_c_im[l])
    w = dict(
        norm_mix_g=norm_mix_g[l][None, :], w_in=w_in[l].astype(BF16),
        cw=lru_conv_w[l], cb=lru_conv_b[l][None, :],
        wax=jnp.concatenate([lru_wa[l], lru_wx[l]], axis=-1).astype(BF16),
        bax=jnp.concatenate([lru_ba[l], lru_bx[l]], axis=-1)[:, None, :],
        lam=lru_lambda[l][None, :], proj=lru_proj[l].astype(BF16),
        ca_re=ca_re, ca_im=ca_im, wb=wb, wc=wc, d=s5_d[l].reshape(1, D_S5),
        wv=s5_glu_wv[l].astype(BF16), wg=s5_glu_wg[l].astype(BF16),
        w_out=w_out[l].astype(BF16), norm_ffn_g=norm_ffn_g[l][None, :],
        ffn_wg=ffn_w_gate[l].astype(BF16), ffn_wu=ffn_w_up[l].astype(BF16),
        ffn_wd=ffn_w_down[l].astype(BF16),
        norm_final_g=norm_final_g[None, :],
    )
    zeros = lambda *s: jnp.zeros(s, F32)
    yp, conv_p, h_p, sr_p, si_p = _layer(
        x_prompt, zeros(batch, CONV_W - 1, D_LRU), zeros(batch, D_LRU),
        zeros(batch, S5_N), zeros(batch, S5_N), w,
        bt_in=256, bt_lru=128, bt_s5=64, bt_merge=128, bm_ffn=512, bf=512, first_pos_is_zero=True)
    ys, conv_s, h_s, sr_s, si_s = _layer(
        x_sample.reshape(dec_batch, D_MODEL), state_lru_conv[l], state_lru_h[l],
        state_s5_re[l].reshape(dec_batch, S5_N), state_s5_im[l].reshape(dec_batch, S5_N), w,
        bt_in=1, bt_lru=1, bt_s5=1, bt_merge=1, bm_ffn=dec_batch, bf=512, first_pos_is_zero=False)
    st = lambda a, n: a.reshape(1, n, S5_G, S5_P)
    return (yp.reshape(batch, seq, D_MODEL), ys.reshape(dec_batch, 1, D_MODEL),
            conv_p[None], h_p[None], st(sr_p, batch), st(si_p, batch),
            conv_s[None], h_s[None], st(sr_s, dec_batch), st(si_s, dec_batch))
```

```python
import functools
import math

import jax
import jax.numpy as jnp
from jax import lax
from jax.experimental import pallas as pl
from jax.experimental.pallas import tpu as pltpu

F32 = jnp.float32
BF16 = jnp.bfloat16

D_MODEL = 2048
D_LRU = D_MODEL
H_LRU = 8
LRU_BW = D_LRU // H_LRU
LRU_C = 8.0
CONV_W = 4
D_S5 = D_MODEL // 2
S5_H = 16
S5_G = D_S5 // S5_H
S5_P = 64
S5_N = S5_G * S5_P
D_FF = 5632
IN_WIDTH = D_LRU + D_S5 + 2 * D_MODEL
EPS = 1e-6

S5_SLAB_G = 8
S5_SLABS = S5_G // S5_SLAB_G
S5_SLAB_IN = S5_SLAB_G * S5_H
S5_SLAB_ST = S5_SLAB_G * S5_P

LANES = 128
SUBLANES = 8
HALF = SUBLANES // 2
VMEM_LIMIT_BYTES = 56 * 1024 * 1024


def _params(sem):
    return pltpu.CompilerParams(dimension_semantics=sem, vmem_limit_bytes=VMEM_LIMIT_BYTES)


def _const_spec(shape):
    nd = len(shape)
    return pl.BlockSpec(shape, lambda *_: (0,) * nd, pipeline_mode=pl.Buffered(1))


def _side_cast_specs(arrs, n_steps, step_of):
    specs, shapes = [], []
    for a in arrs:
        rows = a.shape[0] // n_steps
        assert rows * n_steps == a.shape[0] and rows % (2 * SUBLANES) == 0
        specs.append(pl.BlockSpec((rows, a.shape[1]), lambda *g: (step_of(*g), 0)))
        shapes.append(jax.ShapeDtypeStruct(a.shape, BF16))
    return specs, shapes


def _split_refs(refs, n_in, n_out, n_cast):
    cuts = [n_in, n_in + n_cast, n_in + n_cast + n_out, n_in + 2 * n_cast + n_out]
    parts = [refs[a:b] for a, b in zip([0] + cuts, cuts + [len(refs)])]
    return parts


def _rmsnorm(x, g):
    ms = jnp.mean(x * x, axis=-1, keepdims=True)
    return x * lax.rsqrt(ms + EPS) * g


def _lo_rows(shape):
    return lax.broadcasted_iota(jnp.int32, shape, 0) % SUBLANES < HALF


def _s5prep_kernel(lre_ref, lim_ref, ldt_ref, br_ref, bi_ref,
                   bbr_ref, bbi_ref, abbr_ref, abbi_ref, abr_ref, abi_ref, a2r_ref, a2i_ref):
    lre = lre_ref[...]
    lim = lim_ref[...]
    dt = jnp.exp(ldt_ref[...])
    mag = jnp.exp(lre * dt)
    ab_re = mag * jnp.cos(lim * dt)
    ab_im = mag * jnp.sin(lim * dt)
    e_re = ab_re - 1.0
    e_im = ab_im
    den = lre * lre + lim * lim
    co_re = (e_re * lre + e_im * lim) / den
    co_im = (e_im * lre - e_re * lim) / den
    br = br_ref[...]
    bi = bi_ref[...]
    bb_re = co_re * br - co_im * bi
    bb_im = co_re * bi + co_im * br
    bbr_ref[...] = bb_re
    bbi_ref[...] = bb_im
    abbr_ref[...] = ab_re * bb_re - ab_im * bb_im
    abbi_ref[...] = ab_re * bb_im + ab_im * bb_re
    abr_ref[...] = ab_re
    abi_ref[...] = ab_im
    a2r_ref[...] = ab_re * ab_re - ab_im * ab_im
    a2i_ref[...] = 2.0 * (ab_re * ab_im)


def _s5_prepare(lam_re, lam_im, log_dt, b_re, b_im, c_re, c_im):
    full = (S5_G, S5_P, S5_H)
    flat = (S5_G * S5_P * S5_H // LANES, LANES)
    bc = lambda a: jnp.broadcast_to(a, full).reshape(flat)
    args = (bc(lam_re[:, :, None]), bc(lam_im[:, :, None]), bc(log_dt[:, None, None]),
            b_re.reshape(flat), b_im.reshape(flat))
    sds = jax.ShapeDtypeStruct(flat, F32)
    bbr, bbi, abbr, abbi, abr, abi, a2r, a2i = pl.pallas_call(
        _s5prep_kernel, out_shape=(sds,) * 8, name="s5prep")(*args)
    per_state = lambda a: a.reshape(full)[:, :, 0].reshape(1, S5_N)
    coeff = lambda a1, a2: jnp.concatenate(
        [jnp.broadcast_to(per_state(a1), (HALF, S5_N)), jnp.broadcast_to(per_state(a2), (HALF, S5_N))], 0)
    ca_re, ca_im = coeff(abr, a2r), coeff(abi, a2i)
    eye = jnp.eye(S5_SLAB_G, dtype=F32)

    def slab_b(re, im):
        b = jnp.stack([re.reshape(full), im.reshape(full)], 0)
        b = b.reshape(2, S5_SLABS, S5_SLAB_G, S5_P, S5_H)
        return jnp.einsum('cjgph,gk->jghckp', b, eye).reshape(S5_SLABS, S5_SLAB_IN, 2 * S5_SLAB_ST)

    wb = jnp.concatenate([slab_b(bbr, bbi), slab_b(abbr, abbi)], axis=1)
    cc = jnp.stack([c_re, -c_im], 0).reshape(2, S5_SLABS, S5_SLAB_G, S5_H, S5_P)
    wc = jnp.einsum('cjghp,gk->jcgpkh', cc, eye).reshape(S5_SLABS, 2 * S5_SLAB_ST, S5_SLAB_IN)
    return ca_re, ca_im, wb.astype(BF16), wc.astype(BF16)


PERM_T = 64


def _perm_matrix(nb, to_time_major):
    n = nb * PERM_T
    tm = jnp.arange(n)
    sm = (tm % nb) * PERM_T + tm // nb
    p = jax.nn.one_hot(sm, n, dtype=BF16)
    return p if to_time_major else p.T


def _row_tile_spec(nb, bt, width):
    if bt == 1:
        return pl.BlockSpec((nb, width), lambda *g: (0, 0))
    return pl.BlockSpec((nb, bt, width), lambda *g: (0, g[0], 0))


N_LRU_BLOCKS = 2
INPROJ_BN = 1024
ZR_WIDTH = IN_WIDTH - D_LRU


def _zr_block(j):
    return jnp.where(j <= N_LRU_BLOCKS, (ZR_WIDTH - D_S5) // INPROJ_BN, j - N_LRU_BLOCKS - 1)


def _inproj_kernel(*refs, nb, bt, n_cast):
    (x_ref, g_ref, perm_ref, w_ref), cast_src, (zl_ref, zr_ref), cast_dst, (u_scr,) = _split_refs(
        refs, 4, 2, n_cast)
    j = pl.program_id(1)

    @pl.when(j == 0)
    def _():
        for src, dst in zip(cast_src, cast_dst):
            dst[...] = src[...].astype(BF16)
        if bt == 1:
            u_scr[...] = _rmsnorm(x_ref[...], g_ref[...]).astype(BF16)
        else:
            for s in range(bt // PERM_T):
                ts = slice(s * PERM_T, (s + 1) * PERM_T)
                xin = jnp.concatenate([x_ref[b, ts, :] for b in range(nb)], axis=0)
                u = _rmsnorm(xin, g_ref[...]).astype(BF16)
                u_scr[s * nb * PERM_T:(s + 1) * nb * PERM_T, :] = jnp.dot(
                    perm_ref[...], u, preferred_element_type=F32).astype(BF16)

    acc = jnp.dot(u_scr[...], w_ref[...], preferred_element_type=F32)

    @pl.when(j < N_LRU_BLOCKS)
    def _():
        zl_ref[...] = acc

    @pl.when(j >= N_LRU_BLOCKS)
    def _():
        zr_ref[...] = acc.astype(BF16)


def _inproj(x, g, w, *, nb, seq_len, bt, cast=()):
    assert bt == 1 or bt % PERM_T == 0
    m, bm, bn = nb * seq_len, nb * bt, INPROJ_BN
    perm = _perm_matrix(nb, True) if bt > 1 else jnp.zeros((SUBLANES, LANES), BF16)
    cast_specs, cast_shapes = _side_cast_specs(cast, m // bm, lambda i, j: i)
    return pl.pallas_call(
        functools.partial(_inproj_kernel, nb=nb, bt=bt, n_cast=len(cast)),
        grid=(m // bm, IN_WIDTH // bn),
        in_specs=[_row_tile_spec(nb, bt, D_MODEL),
                  pl.BlockSpec((1, D_MODEL), lambda i, j: (0, 0)),
                  _const_spec(perm.shape),
                  pl.BlockSpec((D_MODEL, bn), lambda i, j: (0, j))] + cast_specs,
        out_specs=[pl.BlockSpec((bm, bn), lambda i, j: (i, jnp.minimum(j, N_LRU_BLOCKS - 1))),
                   pl.BlockSpec((bm, bn), lambda i, j: (i, _zr_block(j)))] + cast_specs,
        out_shape=[jax.ShapeDtypeStruct((m, D_LRU), F32),
                   jax.ShapeDtypeStruct((m, ZR_WIDTH), BF16)] + cast_shapes,
        scratch_shapes=[pltpu.VMEM((bm, D_MODEL), BF16)],
        compiler_params=_params(("parallel", "arbitrary")),
        name="inproj",
    )(x, g, perm, w, *cast)


def _lru_kernel(*refs, nb, bt, n_tiles, first_pos_is_zero, n_cast):
    ((x_ref, p0_ref, p1_ref, p2_ref, h0_ref, cw_ref, cb_ref, wax_ref, bax_ref, lam_ref, proj_ref),
     cast_src, (ya_ref, n0_ref, n1_ref, n2_ref, ht_ref), cast_dst,
     (xe_scr, a_scr, b_scr, hc_scr, hq_scr)) = _split_refs(refs, 11, 5, n_cast)
    c = pl.program_id(0)
    rows = nb * bt
    pad = 2 * SUBLANES
    pipelined = bt > 1

    @pl.when(c == 0)
    def _():
        n0_ref[...] = p0_ref[...]
        n1_ref[...] = p1_ref[...]
        n2_ref[...] = p2_ref[...]
        ht_ref[...] = h0_ref[...]
        if pipelined:
            xe_scr[:pad - 3 * nb, :] = jnp.zeros((pad - 3 * nb, D_LRU), F32)
            hq_scr[...] = jnp.zeros(hq_scr.shape, BF16)

    nlam = -lam_ref[...]
    sp = jnp.maximum(nlam, 0.0) + jnp.log1p(jnp.exp(-jnp.abs(nlam)))

    if bt > 1:
        xe_scr[pad - 3 * nb:pad - 2 * nb, :] = n0_ref[...]
        xe_scr[pad - 2 * nb:pad - nb, :] = n1_ref[...]
        xe_scr[pad - nb:pad, :] = n2_ref[...]
        xe_scr[pad:, :] = x_ref[...]
        is_first = (lax.broadcasted_iota(jnp.int32, (rows, LRU_BW), 0) < nb) & (c == 0)

    for hb in range(H_LRU):
        cs = slice(hb * LRU_BW, (hb + 1) * LRU_BW)
        if pipelined:
            ya = jnp.dot(hq_scr[...], proj_ref[:, cs], preferred_element_type=F32)
            ya_ref[:, cs] = ya.astype(BF16)
        if hb < n_cast:
            cast_dst[hb][...] = cast_src[hb][...].astype(BF16)
        if bt > 1:
            xs = pltpu.roll(xe_scr[:, cs], nb, axis=0)
            xc = (cb_ref[:, cs] + xs[pad - 2 * nb:pad + rows - 2 * nb] * cw_ref[0:1, cs]
                  + xe_scr[pad - 2 * nb:pad + rows - 2 * nb, cs] * cw_ref[1:2, cs]
                  + xs[pad:] * cw_ref[2:3, cs]
                  + xe_scr[pad:, cs] * cw_ref[3:4, cs])
        else:
            xc = (cb_ref[:, cs] + n0_ref[:, cs] * cw_ref[0:1, cs] + n1_ref[:, cs] * cw_ref[1:2, cs]
                  + n2_ref[:, cs] * cw_ref[2:3, cs] + x_ref[:, cs] * cw_ref[3:4, cs])
        gates = jnp.dot(xc.astype(BF16), wax_ref[hb], preferred_element_type=F32) + bax_ref[hb]
        r = jax.nn.sigmoid(gates[:, :LRU_BW])
        i = jax.nn.sigmoid(gates[:, LRU_BW:])
        log_a = (-LRU_C) * r * sp[:, cs]
        a = jnp.exp(log_a)
        mult = jnp.sqrt(jnp.tanh(-log_a) * (1.0 + a * a))
        if first_pos_is_zero:
            mult = jnp.where(is_first, 1.0, mult)
        a_scr[:, cs] = a
        b_scr[:, cs] = mult * (i * xc)

    if not pipelined:
        n0_ref[...] = n1_ref[...]
        n1_ref[...] = n2_ref[...]
        n2_ref[...] = x_ref[...]
        h = a_scr[...] * ht_ref[...] + b_scr[...]
        ht_ref[...] = h
        ya = jnp.dot(h.astype(BF16), proj_ref[...], preferred_element_type=F32)
        ya_ref[...] = ya.astype(BF16)
        return

    @pl.when(c < n_tiles)
    def _():
        n0_ref[...] = x_ref[rows - 3 * nb:rows - 2 * nb, :]
        n1_ref[...] = x_ref[rows - 2 * nb:rows - nb, :]
        n2_ref[...] = x_ref[rows - nb:, :]

        n_cb = D_LRU // LANES
        lo = _lo_rows((SUBLANES, LANES))
        hc_scr[:HALF, :] = ht_ref[...]
        hc_scr[HALF:, :] = ht_ref[...]

        def pair(k, hs):
            rs = pl.ds(pl.multiple_of(k * SUBLANES, SUBLANES), SUBLANES)
            out = []
            for cb in range(n_cb):
                cl = slice(cb * LANES, (cb + 1) * LANES)
                av, bv = a_scr[rs, cl], b_scr[rs, cl]
                first = av * hs[cb] + bv
                second = av * pltpu.roll(first, HALF, axis=0) + bv
                h = jnp.where(lo, first, second)
                b_scr[rs, cl] = h
                out.append(pltpu.roll(h, HALF, axis=0))
            return tuple(out)

        hs = tuple(hc_scr[:, cb * LANES:(cb + 1) * LANES] for cb in range(n_cb))
        hs = lax.fori_loop(0, rows // SUBLANES, pair, hs)
        for cb in range(n_cb):
            hc_scr[:, cb * LANES:(cb + 1) * LANES] = hs[cb]
        ht_ref[...] = hc_scr[:HALF, :]
        hq_scr[...] = b_scr[...].astype(BF16)


def _lru(z, conv0, h0, cw, cb, wax, bax, lam, proj, *, nb, seq_len, bt, first_pos_is_zero, cast=()):
    assert bt == 1 or (nb == HALF and bt % 2 == 0 and bt >= 4)
    assert len(cast) <= H_LRU
    rows, n_tiles = nb * bt, seq_len // bt
    pipelined = bt > 1
    p0, p1, p2 = conv0[:, 0, :], conv0[:, 1, :], conv0[:, 2, :]
    st = pl.BlockSpec((nb, D_LRU), lambda c: (0, 0))
    st_sds = jax.ShapeDtypeStruct((nb, D_LRU), F32)
    x_tile = pl.BlockSpec((rows, D_LRU), lambda c: (jnp.minimum(c, n_tiles - 1), 0))
    ya_tile = pl.BlockSpec((rows, D_LRU), lambda c: (jnp.maximum(c - int(pipelined), 0), 0))
    kern = functools.partial(_lru_kernel, nb=nb, bt=bt, n_tiles=n_tiles,
                             first_pos_is_zero=first_pos_is_zero, n_cast=len(cast))
    cast_specs, cast_shapes = _side_cast_specs(cast, n_tiles, lambda c: jnp.minimum(c, n_tiles - 1))
    small = (SUBLANES, LANES)
    ya, n0, n1, n2, ht, *cast_out = pl.pallas_call(
        kern,
        grid=(n_tiles + int(pipelined),),
        in_specs=[x_tile, st, st, st, st,
                  _const_spec((CONV_W, D_LRU)), _const_spec((1, D_LRU)),
                  _const_spec((H_LRU, LRU_BW, 2 * LRU_BW)), _const_spec((H_LRU, 1, 2 * LRU_BW)),
                  _const_spec((1, D_LRU)), _const_spec((D_LRU, D_MODEL))] + cast_specs,
        out_specs=[ya_tile, st, st, st, st] + cast_specs,
        out_shape=[jax.ShapeDtypeStruct((nb * seq_len, D_LRU), BF16), st_sds, st_sds, st_sds,
                   st_sds] + cast_shapes,
        scratch_shapes=[pltpu.VMEM((rows + 2 * SUBLANES, D_LRU) if pipelined else small, F32),
                        pltpu.VMEM((rows, D_LRU), F32), pltpu.VMEM((rows, D_LRU), F32),
                        pltpu.VMEM((SUBLANES, D_LRU), F32),
                        pltpu.VMEM((rows, D_LRU) if pipelined else (2 * SUBLANES, LANES), BF16)],
        compiler_params=_params(("arbitrary",)),
        name="lru",
    )(z, p0, p1, p2, h0, cw, cb, wax, bax, lam, proj, *cast)
    return ya, jnp.stack([n0, n1, n2], axis=1), ht, cast_out


def _gelu_tanh(x):
    c = math.sqrt(2.0 / math.pi)
    return 0.5 * x * (1.0 + jnp.tanh(c * (x + 0.044715 * (x * x * x))))


S5_GROUPS = 4
S5_GROUP_SLABS = S5_SLABS // S5_GROUPS
S5_GROUP_COLS = S5_N // S5_GROUPS
GLU_GROUP_COLS = D_MODEL // S5_GROUPS


def _s5_b_proj(ub, up, wb_ref, xr_scr, xi_scr, j):
    js = slice(j * S5_SLAB_IN, (j + 1) * S5_SLAB_IN)
    if up is None:
        w = jnp.dot(ub[:, js], wb_ref[j, :S5_SLAB_IN, :], preferred_element_type=F32)
    else:
        w = jnp.dot(jnp.concatenate([ub[:, js], up[:, js]], axis=1), wb_ref[j],
                    preferred_element_type=F32)
    xr_scr[:, j * S5_SLAB_ST:(j + 1) * S5_SLAB_ST] = w[:, :S5_SLAB_ST]
    xi_scr[:, j * S5_SLAB_ST:(j + 1) * S5_SLAB_ST] = w[:, S5_SLAB_ST:]


def _s5_c_proj(xr_scr, xi_scr, wc_ref, y_scr, j):
    ss = slice(j * S5_SLAB_ST, (j + 1) * S5_SLAB_ST)
    xs = jnp.concatenate([xr_scr[:, ss].astype(BF16), xi_scr[:, ss].astype(BF16)], axis=1)
    y_scr[:, j * S5_SLAB_IN:(j + 1) * S5_SLAB_IN] = jnp.dot(xs, wc_ref[j], preferred_element_type=F32)


def _s5_kernel(*refs, nb, bt, n_tiles, n_cast):
    ((u_ref, s0r_ref, s0i_ref, car_ref, cai_ref, wb_ref, wc_ref, d_ref, wv_ref, wg_ref), cast_src,
     (yb_ref, sr_ref, si_ref), cast_dst,
     (xr_scr, xi_scr, pr_scr, pi_scr, y_scr, v_scr)) = _split_refs(refs, 10, 3, n_cast)
    c = pl.program_id(0)
    rows = nb * bt
    ub = u_ref[...]
    u = ub.astype(F32)

    if bt == 1:
        for j in range(S5_SLABS):
            _s5_b_proj(ub, None, wb_ref, xr_scr, xi_scr, j)
        ar, ai = car_ref[0:1, :], cai_ref[0:1, :]
        pr, pi = s0r_ref[...], s0i_ref[...]
        nr = ar * pr - ai * pi + xr_scr[...]
        ni = ar * pi + ai * pr + xi_scr[...]
        xr_scr[...] = nr
        xi_scr[...] = ni
        sr_ref[...] = nr
        si_ref[...] = ni
        for j in range(S5_SLABS):
            _s5_c_proj(xr_scr, xi_scr, wc_ref, y_scr, j)
        v = _gelu_tanh(y_scr[...] + d_ref[...] * u).astype(BF16)
        yv = jnp.dot(v, wv_ref[...], preferred_element_type=F32)
        yg = jnp.dot(v, wg_ref[...], preferred_element_type=F32)
        yb_ref[...] = (yv * jax.nn.sigmoid(yg)).astype(BF16)
        return

    @pl.when(c == 0)
    def _():
        pr_scr[:HALF, :] = s0r_ref[...]
        pr_scr[HALF:, :] = s0r_ref[...]
        pi_scr[:HALF, :] = s0i_ref[...]
        pi_scr[HALF:, :] = s0i_ref[...]
        v_scr[...] = jnp.zeros(v_scr.shape, BF16)

    up = jnp.where(_lo_rows(u.shape), 0.0, pltpu.roll(u, HALF, axis=0)).astype(BF16)
    lo = _lo_rows((SUBLANES, LANES))
    v_prev = v_scr[...]
    for grp in range(S5_GROUPS):
        gs = slice(grp * GLU_GROUP_COLS, (grp + 1) * GLU_GROUP_COLS)
        yv = jnp.dot(v_prev, wv_ref[:, gs], preferred_element_type=F32)
        yg = jnp.dot(v_prev, wg_ref[:, gs], preferred_element_type=F32)
        yb_ref[:, gs] = (yv * jax.nn.sigmoid(yg)).astype(BF16)
        if grp < n_cast:
            cast_dst[grp][...] = cast_src[grp][...].astype(BF16)

        slabs = range(grp * S5_GROUP_SLABS, (grp + 1) * S5_GROUP_SLABS)
        for j in slabs:
            _s5_b_proj(ub, up, wb_ref, xr_scr, xi_scr, j)
        cols = [slice(grp * S5_GROUP_COLS + cb * LANES, grp * S5_GROUP_COLS + (cb + 1) * LANES)
                for cb in range(S5_GROUP_COLS // LANES)]
        carry = [(pr_scr[:, cl], pi_scr[:, cl]) for cl in cols]
        for k in range(rows // SUBLANES):
            rs = slice(k * SUBLANES, (k + 1) * SUBLANES)
            for cb, cl in enumerate(cols):
                pr, pi = carry[cb]
                ar, ai = car_ref[:, cl], cai_ref[:, cl]
                nr = ar * pr - ai * pi + xr_scr[rs, cl]
                ni = ar * pi + ai * pr + xi_scr[rs, cl]
                xr_scr[rs, cl] = nr
                xi_scr[rs, cl] = ni
                carry[cb] = (jnp.where(lo, pltpu.roll(nr, HALF, axis=0), nr),
                             jnp.where(lo, pltpu.roll(ni, HALF, axis=0), ni))
        for cb, cl in enumerate(cols):
            pr_scr[:, cl], pi_scr[:, cl] = carry[cb]
        for j in slabs:
            _s5_c_proj(xr_scr, xi_scr, wc_ref, y_scr, j)
    v_scr[...] = _gelu_tanh(y_scr[...] + d_ref[...] * u).astype(BF16)

    @pl.when(c == n_tiles - 1)
    def _():
        sr_ref[...] = pr_scr[:HALF, :]
        si_ref[...] = pi_scr[:HALF, :]


def _s5(z, s0r, s0i, ca_re, ca_im, wb, wc, d, wv, wg, *, nb, seq_len, bt, cast=()):
    assert bt == 1 or (nb == HALF and bt % 2 == 0)
    assert len(cast) <= S5_GROUPS
    rows, n_tiles = nb * bt, seq_len // bt
    pipelined = bt > 1
    st = pl.BlockSpec((nb, S5_N), lambda c: (0, 0))
    st_sds = jax.ShapeDtypeStruct((nb, S5_N), F32)
    x_s5_block = (ZR_WIDTH - D_S5) // D_S5
    cast_specs, cast_shapes = _side_cast_specs(cast, n_tiles, lambda c: jnp.minimum(c, n_tiles - 1))
    yb, sr, si, *cast_out = pl.pallas_call(
        functools.partial(_s5_kernel, nb=nb, bt=bt, n_tiles=n_tiles, n_cast=len(cast)),
        grid=(n_tiles + int(pipelined),),
        in_specs=[pl.BlockSpec((rows, D_S5), lambda c: (jnp.minimum(c, n_tiles - 1), x_s5_block)),
                  st, st, _const_spec((SUBLANES, S5_N)), _const_spec((SUBLANES, S5_N)),
                  _const_spec((S5_SLABS, 2 * S5_SLAB_IN, 2 * S5_SLAB_ST)),
                  _const_spec((S5_SLABS, 2 * S5_SLAB_ST, S5_SLAB_IN)),
                  _const_spec((1, D_S5)), _const_spec((D_S5, D_MODEL)),
                  _const_spec((D_S5, D_MODEL))] + cast_specs,
        out_specs=[pl.BlockSpec((rows, D_MODEL), lambda c: (jnp.maximum(c - int(pipelined), 0), 0)),
                   st, st] + cast_specs,
        out_shape=[jax.ShapeDtypeStruct((nb * seq_len, D_MODEL), BF16), st_sds, st_sds] + cast_shapes,
        scratch_shapes=[pltpu.VMEM((rows, S5_N), F32), pltpu.VMEM((rows, S5_N), F32),
                        pltpu.VMEM((SUBLANES, S5_N), F32), pltpu.VMEM((SUBLANES, S5_N), F32),
                        pltpu.VMEM((rows, D_S5), F32), pltpu.VMEM((rows, D_S5), BF16)],
        compiler_params=_params(("arbitrary",)),
        name="s5",
    )(z, s0r, s0i, ca_re, ca_im, wb, wc, d, wv, wg, *cast)
    return yb, sr, si, cast_out


def _merge_kernel(ya_ref, yb_ref, ga_ref, gb_ref, x_ref, perm_ref, w_ref, g_ref, x1_ref, u2_ref,
                  *, nb, bt):
    def gated(rs):
        f32 = lambda ref: ref[rs, :].astype(F32)
        merged = jax.nn.sigmoid(f32(ga_ref)) * f32(ya_ref) + jax.nn.sigmoid(f32(gb_ref)) * f32(yb_ref)
        return merged.astype(BF16)

    if bt == 1:
        x1 = x_ref[...] + jnp.dot(gated(slice(None)), w_ref[...], preferred_element_type=F32)
        x1_ref[...] = x1
        u2_ref[...] = _rmsnorm(x1, g_ref[...]).astype(BF16)
        return
    for s in range(bt // PERM_T):
        ts = slice(s * PERM_T, (s + 1) * PERM_T)
        mb = jnp.dot(perm_ref[...], gated(slice(s * nb * PERM_T, (s + 1) * nb * PERM_T)),
                     preferred_element_type=F32).astype(BF16)
        delta = jnp.dot(mb, w_ref[...], preferred_element_type=F32)
        for b in range(nb):
            x1 = x_ref[b, ts, :] + delta[b * PERM_T:(b + 1) * PERM_T]
            x1_ref[b, ts, :] = x1
            u2_ref[b, ts, :] = _rmsnorm(x1, g_ref[...]).astype(BF16)


def _merge(ya, yb, zr, x, w_out, g, *, nb, seq_len, bt):
    assert bt == 1 or bt % PERM_T == 0
    bm = nb * bt
    row = lambda col: pl.BlockSpec((bm, D_MODEL), lambda i: (i, col))
    perm = _perm_matrix(nb, False) if bt > 1 else jnp.zeros((SUBLANES, LANES), BF16)
    tile = _row_tile_spec(nb, bt, D_MODEL)
    return pl.pallas_call(
        functools.partial(_merge_kernel, nb=nb, bt=bt),
        grid=(seq_len // bt,),
        in_specs=[row(0), row(0), row(0), row(1), tile, _const_spec(perm.shape),
                  _const_spec((D_MODEL, D_MODEL)), _const_spec((1, D_MODEL))],
        out_specs=(tile, tile),
        out_shape=(jax.ShapeDtypeStruct(x.shape, F32), jax.ShapeDtypeStruct(x.shape, BF16)),
        compiler_params=_params(("parallel",)),
        name="merge",
    )(ya, yb, zr, zr, x, perm, w_out, g)


def _ffn_kernel(u_ref, x1_ref, wg_ref, wu_ref, wd_ref, g_ref, y_ref, acc_scr):
    f = pl.program_id(1)

    @pl.when(f == 0)
    def _():
        acc_scr[...] = x1_ref[...]

    u = u_ref[...]
    hg = jnp.dot(u, wg_ref[...], preferred_element_type=F32)
    hu = jnp.dot(u, wu_ref[...], preferred_element_type=F32)
    hid = (hg * jax.nn.sigmoid(hg) * hu).astype(BF16)
    acc_scr[...] += jnp.dot(hid, wd_ref[...], preferred_element_type=F32)

    @pl.when(f == pl.num_programs(1) - 1)
    def _():
        y_ref[...] = _rmsnorm(acc_scr[...], g_ref[...])


def _ffn(u2, x1, wg, wu, wd, g, bm, bf):
    m = x1.shape[0]
    return pl.pallas_call(
        _ffn_kernel,
        grid=(m // bm, D_FF // bf),
        in_specs=[pl.BlockSpec((bm, D_MODEL), lambda i, f: (i, 0)),
                  pl.BlockSpec((bm, D_MODEL), lambda i, f: (i, 0)),
                  pl.BlockSpec((D_MODEL, bf), lambda i, f: (0, f)),
                  pl.BlockSpec((D_MODEL, bf), lambda i, f: (0, f)),
                  pl.BlockSpec((bf, D_MODEL), lambda i, f: (f, 0)),
                  pl.BlockSpec((1, D_MODEL), lambda i, f: (0, 0))],
        out_specs=pl.BlockSpec((bm, D_MODEL), lambda i, f: (i, 0)),
        out_shape=jax.ShapeDtypeStruct((m, D_MODEL), F32),
        scratch_shapes=[pltpu.VMEM((bm, D_MODEL), F32)],
        compiler_params=_params(("parallel", "arbitrary")),
        name="ffn",
    )(u2, x1, wg, wu, wd, g)


def _layer(x, conv0, h0, s0r, s0i, w, w32, *, bt_in, bt_lru, bt_s5, bt_merge, bm_ffn, bf,
           first_pos_is_zero):
    nb, seq_len = (x.shape[0], 1) if x.ndim == 2 else x.shape[:2]
    seq = dict(nb=nb, seq_len=seq_len)
    w = dict(w)
    take = lambda *names: tuple(w32[n] for n in names) if w32 else ()
    keep = lambda names, arrs: w.update(zip(names, arrs)) if w32 else None

    zl, zr, *cast = _inproj(x, w['norm_mix_g'], w['w_in'], bt=bt_in, cast=take('proj'), **seq)
    keep(('proj',), cast)
    ya, conv_n, h_n, cast = _lru(zl, conv0, h0, w['cw'], w['cb'], w['wax'], w['bax'], w['lam'],
                                 w['proj'], bt=bt_lru, first_pos_is_zero=first_pos_is_zero,
                                 cast=take('w_out', 'wv', 'wg'), **seq)
    keep(('w_out', 'wv', 'wg'), cast)
    yb, sr, si, cast = _s5(zr, s0r, s0i, w['ca_re'], w['ca_im'], w['wb'], w['wc'], w['d'], w['wv'],
                           w['wg'], bt=bt_s5, cast=take('ffn_wg', 'ffn_wu', 'ffn_wd'), **seq)
    keep(('ffn_wg', 'ffn_wu', 'ffn_wd'), cast)
    x1, u2 = _merge(ya, yb, zr, x, w['w_out'], w['norm_ffn_g'], bt=bt_merge, **seq)
    m = nb * seq_len
    y = _ffn(u2.reshape(m, D_MODEL), x1.reshape(m, D_MODEL), w['ffn_wg'], w['ffn_wu'], w['ffn_wd'],
             w['norm_final_g'], bm_ffn, bf)
    return (y, conv_n, h_n, sr, si), w


def kernel(x_prompt, x_sample, state_lru_conv, state_lru_h, state_s5_re, state_s5_im, norm_mix_g, w_in, lru_conv_w, lru_conv_b, lru_wa, lru_ba, lru_wx, lru_bx, lru_lambda, lru_proj, s5_lambda_re, s5_lambda_im, s5_log_dt, s5_b_re, s5_b_im, s5_c_re, s5_c_im, s5_d, s5_glu_wv, s5_glu_wg, w_out, norm_ffn_g, ffn_w_gate, ffn_w_up, ffn_w_down, norm_final_g):
    batch, seq, _ = x_prompt.shape
    dec_batch = x_sample.shape[0]
    l = 0
    ca_re, ca_im, wb, wc = _s5_prepare(s5_lambda_re[l], s5_lambda_im[l], s5_log_dt[l],
                                       s5_b_re[l], s5_b_im[l], s5_c_re[l], s5_c_im[l])
    w = dict(
        norm_mix_g=norm_mix_g[l][None, :], w_in=w_in[l].astype(BF16),
        cw=lru_conv_w[l], cb=lru_conv_b[l][None, :],
        wax=jnp.concatenate([lru_wa[l], lru_wx[l]], axis=-1).astype(BF16),
        bax=jnp.concatenate([lru_ba[l], lru_bx[l]], axis=-1)[:, None, :],
        lam=lru_lambda[l][None, :],
        ca_re=ca_re, ca_im=ca_im, wb=wb, wc=wc, d=s5_d[l].reshape(1, D_S5),
        norm_ffn_g=norm_ffn_g[l][None, :], norm_final_g=norm_final_g[None, :],
    )
    w32 = dict(proj=lru_proj[l], w_out=w_out[l], wv=s5_glu_wv[l], wg=s5_glu_wg[l],
               ffn_wg=ffn_w_gate[l], ffn_wu=ffn_w_up[l], ffn_wd=ffn_w_down[l])
    zeros = lambda *s: jnp.zeros(s, F32)
    (yp, conv_p, h_p, sr_p, si_p), w = _layer(
        x_prompt, zeros(batch, CONV_W - 1, D_LRU), zeros(batch, D_LRU),
        zeros(batch, S5_N), zeros(batch, S5_N), w, w32,
        bt_in=256, bt_lru=128, bt_s5=64, bt_merge=128, bm_ffn=512, bf=512, first_pos_is_zero=True)
    (ys, conv_s, h_s, sr_s, si_s), _ = _layer(
        x_sample.reshape(dec_batch, D_MODEL), state_lru_conv[l], state_lru_h[l],
        state_s5_re[l].reshape(dec_batch, S5_N), state_s5_im[l].reshape(dec_batch, S5_N), w, None,
        bt_in=1, bt_lru=1, bt_s5=1, bt_merge=1, bm_ffn=dec_batch, bf=512, first_pos_is_zero=False)
    st = lambda a, n: a.reshape(1, n, S5_G, S5_P)
    return (yp.reshape(batch, seq, D_MODEL), ys.reshape(dec_batch, 1, D_MODEL),
            conv_p[None], h_p[None], st(sr_p, batch), st(si_p, batch),
            conv_s[None], h_s[None], st(sr_s, dec_batch), st(si_s, dec_batch))
```

```python
import functools
import math

import jax
import jax.numpy as jnp
import numpy as np
from jax import lax
from jax.experimental import pallas as pl
from jax.experimental.pallas import tpu as pltpu

F32 = jnp.float32
BF16 = jnp.bfloat16

D_MODEL = 2048
D_LRU = D_MODEL
H_LRU = 8
LRU_BW = D_LRU // H_LRU
LRU_C = 8.0
CONV_W = 4
D_S5 = D_MODEL // 2
S5_H = 16
S5_G = D_S5 // S5_H
S5_P = 64
S5_N = S5_G * S5_P
D_FF = 5632
IN_WIDTH = D_LRU + D_S5 + 2 * D_MODEL
EPS = 1e-6

S5_SLAB_G = 8
S5_SLABS = S5_G // S5_SLAB_G
S5_SLAB_IN = S5_SLAB_G * S5_H
S5_SLAB_ST = S5_SLAB_G * S5_P

LANES = 128
SUBLANES = 8
HALF = SUBLANES // 2
VMEM_LIMIT_BYTES = 56 * 1024 * 1024


def _params(sem):
    return pltpu.CompilerParams(dimension_semantics=sem, vmem_limit_bytes=VMEM_LIMIT_BYTES)


def _const_spec(shape):
    nd = len(shape)
    return pl.BlockSpec(shape, lambda *_: (0,) * nd, pipeline_mode=pl.Buffered(1))


def _side_cast_specs(arrs, n_steps, step_of):
    specs, shapes = [], []
    for a in arrs:
        rows = a.shape[0] // n_steps
        assert rows * n_steps == a.shape[0] and rows % (2 * SUBLANES) == 0
        specs.append(pl.BlockSpec((rows, a.shape[1]), lambda *g: (step_of(*g), 0)))
        shapes.append(jax.ShapeDtypeStruct(a.shape, BF16))
    return specs, shapes


def _split_refs(refs, n_in, n_out, n_cast):
    cuts = [n_in, n_in + n_cast, n_in + n_cast + n_out, n_in + 2 * n_cast + n_out]
    parts = [refs[a:b] for a, b in zip([0] + cuts, cuts + [len(refs)])]
    return parts


def _rmsnorm(x, g):
    ms = jnp.mean(x * x, axis=-1, keepdims=True)
    return x * lax.rsqrt(ms + EPS) * g


LOG2E = 1.4426950408889634


def _sigmoid(x):
    return 1.0 / (1.0 + jnp.exp2(x * (-LOG2E)))


def _lo_rows(shape):
    return lax.broadcasted_iota(jnp.int32, shape, 0) % SUBLANES < HALF


def _s5prep_kernel(lre_ref, lim_ref, ldt_ref, br_ref, bi_ref,
                   bbr_ref, bbi_ref, abbr_ref, abbi_ref, abr_ref, abi_ref, a2r_ref, a2i_ref):
    lre = lre_ref[...]
    lim = lim_ref[...]
    dt = jnp.exp(ldt_ref[...])
    mag = jnp.exp(lre * dt)
    ab_re = mag * jnp.cos(lim * dt)
    ab_im = mag * jnp.sin(lim * dt)
    e_re = ab_re - 1.0
    e_im = ab_im
    den = lre * lre + lim * lim
    co_re = (e_re * lre + e_im * lim) / den
    co_im = (e_im * lre - e_re * lim) / den
    br = br_ref[...]
    bi = bi_ref[...]
    bb_re = co_re * br - co_im * bi
    bb_im = co_re * bi + co_im * br
    bbr_ref[...] = bb_re
    bbi_ref[...] = bb_im
    abbr_ref[...] = ab_re * bb_re - ab_im * bb_im
    abbi_ref[...] = ab_re * bb_im + ab_im * bb_re
    abr_ref[...] = ab_re
    abi_ref[...] = ab_im
    a2r_ref[...] = ab_re * ab_re - ab_im * ab_im
    a2i_ref[...] = 2.0 * (ab_re * ab_im)


def _s5_prepare(lam_re, lam_im, log_dt, b_re, b_im, c_re, c_im):
    full = (S5_G, S5_P, S5_H)
    flat = (S5_G * S5_P * S5_H // LANES, LANES)
    bc = lambda a: jnp.broadcast_to(a, full).reshape(flat)
    args = (bc(lam_re[:, :, None]), bc(lam_im[:, :, None]), bc(log_dt[:, None, None]),
            b_re.reshape(flat), b_im.reshape(flat))
    sds = jax.ShapeDtypeStruct(flat, F32)
    bbr, bbi, abbr, abbi, abr, abi, a2r, a2i = pl.pallas_call(
        _s5prep_kernel, out_shape=(sds,) * 8, name="s5prep")(*args)
    per_state = lambda a: a.reshape(full)[:, :, 0].reshape(1, S5_N)
    coeff = lambda a1, a2: jnp.concatenate(
        [jnp.broadcast_to(per_state(a1), (HALF, S5_N)), jnp.broadcast_to(per_state(a2), (HALF, S5_N))], 0)
    ca_re, ca_im = coeff(abr, a2r), coeff(abi, a2i)
    eye = np.eye(S5_SLAB_G, dtype=np.float32)

    def slab_b(re, im):
        b = jnp.stack([re.reshape(full), im.reshape(full)], 0)
        b = b.reshape(2, S5_SLABS, S5_SLAB_G, S5_P, S5_H)
        return jnp.einsum('cjgph,gk->jghckp', b, eye).reshape(S5_SLABS, S5_SLAB_IN, 2 * S5_SLAB_ST)

    wb = jnp.concatenate([slab_b(bbr, bbi), slab_b(abbr, abbi)], axis=1)
    cc = jnp.stack([c_re, -c_im], 0).reshape(2, S5_SLABS, S5_SLAB_G, S5_H, S5_P)
    wc = jnp.einsum('cjghp,gk->jcgpkh', cc, eye).reshape(S5_SLABS, 2 * S5_SLAB_ST, S5_SLAB_IN)
    return ca_re, ca_im, wb.astype(BF16), wc.astype(BF16)


PERM_T = 64


def _perm_matrix(nb, to_time_major):
    n = nb * PERM_T
    tm = np.arange(n)
    sm = (tm % nb) * PERM_T + tm // nb
    p = np.zeros((n, n), np.float32)
    p[tm, sm] = 1.0
    return jnp.asarray(p if to_time_major else p.T, dtype=BF16)


def _row_tile_spec(nb, bt, width):
    if bt == 1:
        return pl.BlockSpec((nb, width), lambda *g: (0, 0))
    return pl.BlockSpec((nb, bt, width), lambda *g: (0, g[0], 0))


N_LRU_BLOCKS = 2
INPROJ_BN = 1024
ZR_WIDTH = IN_WIDTH - D_LRU


def _zr_block(j):
    return jnp.where(j <= N_LRU_BLOCKS, (ZR_WIDTH - D_S5) // INPROJ_BN, j - N_LRU_BLOCKS - 1)


def _inproj_kernel(*refs, nb, bt, n_cast):
    (x_ref, g_ref, perm_ref, w_ref), cast_src, (zl_ref, zr_ref), cast_dst, (u_scr,) = _split_refs(
        refs, 4, 2, n_cast)
    j = pl.program_id(1)

    @pl.when(j == 0)
    def _():
        for src, dst in zip(cast_src, cast_dst):
            dst[...] = src[...].astype(BF16)
        if bt == 1:
            u_scr[...] = _rmsnorm(x_ref[...], g_ref[...]).astype(BF16)
        else:
            for s in range(bt // PERM_T):
                ts = slice(s * PERM_T, (s + 1) * PERM_T)
                xin = jnp.concatenate([x_ref[b, ts, :] for b in range(nb)], axis=0)
                u = _rmsnorm(xin, g_ref[...]).astype(BF16)
                u_scr[s * nb * PERM_T:(s + 1) * nb * PERM_T, :] = jnp.dot(
                    perm_ref[...], u, preferred_element_type=F32).astype(BF16)

    acc = jnp.dot(u_scr[...], w_ref[...], preferred_element_type=F32)

    @pl.when(j < N_LRU_BLOCKS)
    def _():
        zl_ref[...] = acc

    @pl.when(j >= N_LRU_BLOCKS)
    def _():
        zr_ref[...] = acc.astype(BF16)


def _inproj(x, g, w, *, nb, seq_len, bt, cast=()):
    assert bt == 1 or bt % PERM_T == 0
    m, bm, bn = nb * seq_len, nb * bt, INPROJ_BN
    perm = _perm_matrix(nb, True) if bt > 1 else jnp.zeros((SUBLANES, LANES), BF16)
    cast_specs, cast_shapes = _side_cast_specs(cast, m // bm, lambda i, j: i)
    return pl.pallas_call(
        functools.partial(_inproj_kernel, nb=nb, bt=bt, n_cast=len(cast)),
        grid=(m // bm, IN_WIDTH // bn),
        in_specs=[_row_tile_spec(nb, bt, D_MODEL),
                  pl.BlockSpec((1, D_MODEL), lambda i, j: (0, 0)),
                  _const_spec(perm.shape),
                  pl.BlockSpec((D_MODEL, bn), lambda i, j: (0, j))] + cast_specs,
        out_specs=[pl.BlockSpec((bm, bn), lambda i, j: (i, jnp.minimum(j, N_LRU_BLOCKS - 1))),
                   pl.BlockSpec((bm, bn), lambda i, j: (i, _zr_block(j)))] + cast_specs,
        out_shape=[jax.ShapeDtypeStruct((m, D_LRU), F32),
                   jax.ShapeDtypeStruct((m, ZR_WIDTH), BF16)] + cast_shapes,
        scratch_shapes=[pltpu.VMEM((bm, D_MODEL), BF16)],
        compiler_params=_params(("parallel", "arbitrary")),
        name="inproj",
    )(x, g, perm, w, *cast)


def _lru_kernel(*refs, nb, bt, n_tiles, first_pos_is_zero, n_cast):
    ((x_ref, conv0_ref, h0_ref, cw_ref, cb_ref, wax_ref, bax_ref, lam_ref, proj_ref),
     cast_src, (ya_ref, conv_ref, ht_ref), cast_dst,
     (xe_scr, a_scr, b_scr, hc_scr, hq_scr)) = _split_refs(refs, 9, 3, n_cast)
    c = pl.program_id(0)
    rows = nb * bt
    pad = 2 * SUBLANES
    pipelined = bt > 1

    @pl.when(c == 0)
    def _():
        conv_ref[...] = conv0_ref[...]
        ht_ref[...] = h0_ref[...]
        if pipelined:
            xe_scr[:pad - 3 * nb, :] = jnp.zeros((pad - 3 * nb, D_LRU), F32)
            hq_scr[...] = jnp.zeros(hq_scr.shape, BF16)

    nlam = -lam_ref[...]
    sp = jnp.maximum(nlam, 0.0) + jnp.log1p(jnp.exp(-jnp.abs(nlam)))
    k_tanh = LRU_C * sp
    k_exp2 = (-LRU_C * LOG2E) * sp

    if bt > 1:
        for k in range(CONV_W - 1):
            xe_scr[pad - (3 - k) * nb:pad - (2 - k) * nb, :] = conv_ref[:, k, :]
        xe_scr[pad:, :] = x_ref[...]
        is_first = (lax.broadcasted_iota(jnp.int32, (rows, LRU_BW), 0) < nb) & (c == 0)

    for hb in range(H_LRU):
        cs = slice(hb * LRU_BW, (hb + 1) * LRU_BW)
        if pipelined:
            ya = jnp.dot(hq_scr[...], proj_ref[:, cs], preferred_element_type=F32)
            ya_ref[:, cs] = ya.astype(BF16)
        if hb < n_cast:
            cast_dst[hb][...] = cast_src[hb][...].astype(BF16)
        if bt > 1:
            xs = pltpu.roll(xe_scr[:, cs], nb, axis=0)
            xc = (cb_ref[:, cs] + xs[pad - 2 * nb:pad + rows - 2 * nb] * cw_ref[0:1, cs]
                  + xe_scr[pad - 2 * nb:pad + rows - 2 * nb, cs] * cw_ref[1:2, cs]
                  + xs[pad:] * cw_ref[2:3, cs]
                  + xe_scr[pad:, cs] * cw_ref[3:4, cs])
        else:
            xc = (cb_ref[:, cs] + conv_ref[:, 0, cs] * cw_ref[0:1, cs]
                  + conv_ref[:, 1, cs] * cw_ref[1:2, cs] + conv_ref[:, 2, cs] * cw_ref[2:3, cs]
                  + x_ref[:, cs] * cw_ref[3:4, cs])
        gates = jnp.dot(xc.astype(BF16), wax_ref[hb], preferred_element_type=F32) + bax_ref[hb]
        r = _sigmoid(gates[:, :LRU_BW])
        i = _sigmoid(gates[:, LRU_BW:])
        a = jnp.exp2(r * k_exp2[:, cs])
        m2 = jnp.tanh(r * k_tanh[:, cs]) * (1.0 + a * a)
        mult = jnp.where(m2 > 0.0, m2 * lax.rsqrt(m2), 0.0)
        if first_pos_is_zero:
            mult = jnp.where(is_first, 1.0, mult)
        a_scr[:, cs] = a
        b_scr[:, cs] = mult * (i * xc)

    if not pipelined:
        conv_ref[:, 0, :] = conv_ref[:, 1, :]
        conv_ref[:, 1, :] = conv_ref[:, 2, :]
        conv_ref[:, 2, :] = x_ref[...]
        h = a_scr[...] * ht_ref[...] + b_scr[...]
        ht_ref[...] = h
        ya = jnp.dot(h.astype(BF16), proj_ref[...], preferred_element_type=F32)
        ya_ref[...] = ya.astype(BF16)
        return

    @pl.when(c < n_tiles)
    def _():
        for k in range(CONV_W - 1):
            conv_ref[:, k, :] = x_ref[rows - (3 - k) * nb:rows - (2 - k) * nb, :]

        n_cb = D_LRU // LANES
        lo = _lo_rows((SUBLANES, LANES))
        hc_scr[:HALF, :] = ht_ref[...]
        hc_scr[HALF:, :] = ht_ref[...]

        def pair(k, hs):
            rs = pl.ds(pl.multiple_of(k * SUBLANES, SUBLANES), SUBLANES)
            out = []
            for cb in range(n_cb):
                cl = slice(cb * LANES, (cb + 1) * LANES)
                av, bv = a_scr[rs, cl], b_scr[rs, cl]
                first = av * hs[cb] + bv
                second = av * pltpu.roll(first, HALF, axis=0) + bv
                h = jnp.where(lo, first, second)
                b_scr[rs, cl] = h
                out.append(pltpu.roll(h, HALF, axis=0))
            return tuple(out)

        hs = tuple(hc_scr[:, cb * LANES:(cb + 1) * LANES] for cb in range(n_cb))
        hs = lax.fori_loop(0, rows // SUBLANES, pair, hs)
        for cb in range(n_cb):
            hc_scr[:, cb * LANES:(cb + 1) * LANES] = hs[cb]
        ht_ref[...] = hc_scr[:HALF, :]
        hq_scr[...] = b_scr[...].astype(BF16)


def _lru(z, conv0, h0, cw, cb, wax, bax, lam, proj, *, nb, seq_len, bt, first_pos_is_zero, cast=()):
    assert bt == 1 or (nb == HALF and bt % 2 == 0 and bt >= 4)
    assert len(cast) <= H_LRU
    rows, n_tiles = nb * bt, seq_len // bt
    pipelined = bt > 1
    st = pl.BlockSpec((nb, D_LRU), lambda c: (0, 0))
    st_sds = jax.ShapeDtypeStruct((nb, D_LRU), F32)
    cst = pl.BlockSpec((nb, CONV_W - 1, D_LRU), lambda c: (0, 0, 0))
    cst_sds = jax.ShapeDtypeStruct((nb, CONV_W - 1, D_LRU), F32)
    x_tile = pl.BlockSpec((rows, D_LRU), lambda c: (jnp.minimum(c, n_tiles - 1), 0))
    ya_tile = pl.BlockSpec((rows, D_LRU), lambda c: (jnp.maximum(c - int(pipelined), 0), 0))
    kern = functools.partial(_lru_kernel, nb=nb, bt=bt, n_tiles=n_tiles,
                             first_pos_is_zero=first_pos_is_zero, n_cast=len(cast))
    cast_specs, cast_shapes = _side_cast_specs(cast, n_tiles, lambda c: jnp.minimum(c, n_tiles - 1))
    small = (SUBLANES, LANES)
    ya, conv, ht, *cast_out = pl.pallas_call(
        kern,
        grid=(n_tiles + int(pipelined),),
        in_specs=[x_tile, cst, st,
                  _const_spec((CONV_W, D_LRU)), _const_spec((1, D_LRU)),
                  _const_spec((H_LRU, LRU_BW, 2 * LRU_BW)), _const_spec((H_LRU, 1, 2 * LRU_BW)),
                  _const_spec((1, D_LRU)), _const_spec((D_LRU, D_MODEL))] + cast_specs,
        out_specs=[ya_tile, cst, st] + cast_specs,
        out_shape=[jax.ShapeDtypeStruct((nb * seq_len, D_LRU), BF16), cst_sds, st_sds] + cast_shapes,
        scratch_shapes=[pltpu.VMEM((rows + 2 * SUBLANES, D_LRU) if pipelined else small, F32),
                        pltpu.VMEM((rows, D_LRU), F32), pltpu.VMEM((rows, D_LRU), F32),
                        pltpu.VMEM((SUBLANES, D_LRU), F32),
                        pltpu.VMEM((rows, D_LRU) if pipelined else (2 * SUBLANES, LANES), BF16)],
        compiler_params=_params(("arbitrary",)),
        name="lru",
    )(z, conv0, h0, cw, cb, wax, bax, lam, proj, *cast)
    return ya, conv, ht, cast_out


def _gelu_tanh(x):
    c = math.sqrt(2.0 / math.pi)
    return 0.5 * x * (1.0 + jnp.tanh(c * (x + 0.044715 * (x * x * x))))


S5_GROUPS = 4
S5_GROUP_SLABS = S5_SLABS // S5_GROUPS
S5_GROUP_COLS = S5_N // S5_GROUPS
GLU_GROUP_COLS = D_MODEL // S5_GROUPS


def _s5_b_proj(ub, up, wb_ref, xr_scr, xi_scr, j):
    js = slice(j * S5_SLAB_IN, (j + 1) * S5_SLAB_IN)
    if up is None:
        w = jnp.dot(ub[:, js], wb_ref[j, :S5_SLAB_IN, :], preferred_element_type=F32)
    else:
        w = jnp.dot(jnp.concatenate([ub[:, js], up[:, js]], axis=1), wb_ref[j],
                    preferred_element_type=F32)
    xr_scr[:, j * S5_SLAB_ST:(j + 1) * S5_SLAB_ST] = w[:, :S5_SLAB_ST]
    xi_scr[:, j * S5_SLAB_ST:(j + 1) * S5_SLAB_ST] = w[:, S5_SLAB_ST:]


def _s5_c_proj(xr_scr, xi_scr, wc_ref, y_scr, j):
    ss = slice(j * S5_SLAB_ST, (j + 1) * S5_SLAB_ST)
    xs = jnp.concatenate([xr_scr[:, ss].astype(BF16), xi_scr[:, ss].astype(BF16)], axis=1)
    y_scr[:, j * S5_SLAB_IN:(j + 1) * S5_SLAB_IN] = jnp.dot(xs, wc_ref[j], preferred_element_type=F32)


def _s5_kernel(*refs, nb, bt, n_tiles, n_cast):
    ((u_ref, s0r_ref, s0i_ref, car_ref, cai_ref, wb_ref, wc_ref, d_ref, wv_ref, wg_ref), cast_src,
     (yb_ref, sr_ref, si_ref), cast_dst,
     (xr_scr, xi_scr, pr_scr, pi_scr, y_scr, v_scr)) = _split_refs(refs, 10, 3, n_cast)
    c = pl.program_id(0)
    rows = nb * bt
    ub = u_ref[...]
    u = ub.astype(F32)

    if bt == 1:
        for j in range(S5_SLABS):
            _s5_b_proj(ub, None, wb_ref, xr_scr, xi_scr, j)
        ar, ai = car_ref[0:1, :], cai_ref[0:1, :]
        pr, pi = s0r_ref[...], s0i_ref[...]
        nr = ar * pr - ai * pi + xr_scr[...]
        ni = ar * pi + ai * pr + xi_scr[...]
        xr_scr[...] = nr
        xi_scr[...] = ni
        sr_ref[...] = nr
        si_ref[...] = ni
        for j in range(S5_SLABS):
            _s5_c_proj(xr_scr, xi_scr, wc_ref, y_scr, j)
        v = _gelu_tanh(y_scr[...] + d_ref[...] * u).astype(BF16)
        yv = jnp.dot(v, wv_ref[...], preferred_element_type=F32)
        yg = jnp.dot(v, wg_ref[...], preferred_element_type=F32)
        yb_ref[...] = (yv * _sigmoid(yg)).astype(BF16)
        return

    @pl.when(c == 0)
    def _():
        pr_scr[:HALF, :] = s0r_ref[...]
        pr_scr[HALF:, :] = s0r_ref[...]
        pi_scr[:HALF, :] = s0i_ref[...]
        pi_scr[HALF:, :] = s0i_ref[...]
        v_scr[...] = jnp.zeros(v_scr.shape, BF16)

    up = jnp.where(_lo_rows(u.shape), 0.0, pltpu.roll(u, HALF, axis=0)).astype(BF16)
    lo = _lo_rows((SUBLANES, LANES))
    v_prev = v_scr[...]
    for grp in range(S5_GROUPS):
        gs = slice(grp * GLU_GROUP_COLS, (grp + 1) * GLU_GROUP_COLS)
        yv = jnp.dot(v_prev, wv_ref[:, gs], preferred_element_type=F32)
        yg = jnp.dot(v_prev, wg_ref[:, gs], preferred_element_type=F32)
        yb_ref[:, gs] = (yv * _sigmoid(yg)).astype(BF16)
        if grp < n_cast:
            cast_dst[grp][...] = cast_src[grp][...].astype(BF16)

        slabs = range(grp * S5_GROUP_SLABS, (grp + 1) * S5_GROUP_SLABS)
        for j in slabs:
            _s5_b_proj(ub, up, wb_ref, xr_scr, xi_scr, j)
        cols = [slice(grp * S5_GROUP_COLS + cb * LANES, grp * S5_GROUP_COLS + (cb + 1) * LANES)
                for cb in range(S5_GROUP_COLS // LANES)]
        carry = [(pr_scr[:, cl], pi_scr[:, cl]) for cl in cols]
        for k in range(rows // SUBLANES):
            rs = slice(k * SUBLANES, (k + 1) * SUBLANES)
            for cb, cl in enumerate(cols):
                pr, pi = carry[cb]
                ar, ai = car_ref[:, cl], cai_ref[:, cl]
                nr = ar * pr - ai * pi + xr_scr[rs, cl]
                ni = ar * pi + ai * pr + xi_scr[rs, cl]
                xr_scr[rs, cl] = nr
                xi_scr[rs, cl] = ni
                carry[cb] = (jnp.where(lo, pltpu.roll(nr, HALF, axis=0), nr),
                             jnp.where(lo, pltpu.roll(ni, HALF, axis=0), ni))
        for cb, cl in enumerate(cols):
            pr_scr[:, cl], pi_scr[:, cl] = carry[cb]
        for j in slabs:
            _s5_c_proj(xr_scr, xi_scr, wc_ref, y_scr, j)
    v_scr[...] = _gelu_tanh(y_scr[...] + d_ref[...] * u).astype(BF16)

    @pl.when(c == n_tiles - 1)
    def _():
        sr_ref[...] = pr_scr[:HALF, :]
        si_ref[...] = pi_scr[:HALF, :]


def _s5(z, s0r, s0i, ca_re, ca_im, wb, wc, d, wv, wg, *, nb, seq_len, bt, cast=()):
    assert bt == 1 or (nb == HALF and bt % 2 == 0)
    assert len(cast) <= S5_GROUPS
    rows, n_tiles = nb * bt, seq_len // bt
    pipelined = bt > 1
    st = pl.BlockSpec((nb, S5_N), lambda c: (0, 0))
    st_sds = jax.ShapeDtypeStruct((nb, S5_N), F32)
    x_s5_block = (ZR_WIDTH - D_S5) // D_S5
    cast_specs, cast_shapes = _side_cast_specs(cast, n_tiles, lambda c: jnp.minimum(c, n_tiles - 1))
    yb, sr, si, *cast_out = pl.pallas_call(
        functools.partial(_s5_kernel, nb=nb, bt=bt, n_tiles=n_tiles, n_cast=len(cast)),
        grid=(n_tiles + int(pipelined),),
        in_specs=[pl.BlockSpec((rows, D_S5), lambda c: (jnp.minimum(c, n_tiles - 1), x_s5_block)),
                  st, st, _const_spec((SUBLANES, S5_N)), _const_spec((SUBLANES, S5_N)),
                  _const_spec((S5_SLABS, 2 * S5_SLAB_IN, 2 * S5_SLAB_ST)),
                  _const_spec((S5_SLABS, 2 * S5_SLAB_ST, S5_SLAB_IN)),
                  _const_spec((1, D_S5)), _const_spec((D_S5, D_MODEL)),
                  _const_spec((D_S5, D_MODEL))] + cast_specs,
        out_specs=[pl.BlockSpec((rows, D_MODEL), lambda c: (jnp.maximum(c - int(pipelined), 0), 0)),
                   st, st] + cast_specs,
        out_shape=[jax.ShapeDtypeStruct((nb * seq_len, D_MODEL), BF16), st_sds, st_sds] + cast_shapes,
        scratch_shapes=[pltpu.VMEM((rows, S5_N), F32), pltpu.VMEM((rows, S5_N), F32),
                        pltpu.VMEM((SUBLANES, S5_N), F32), pltpu.VMEM((SUBLANES, S5_N), F32),
                        pltpu.VMEM((rows, D_S5), F32), pltpu.VMEM((rows, D_S5), BF16)],
        compiler_params=_params(("arbitrary",)),
        name="s5",
    )(z, s0r, s0i, ca_re, ca_im, wb, wc, d, wv, wg, *cast)
    return yb, sr, si, cast_out


def _merge_kernel(ya_ref, yb_ref, ga_ref, gb_ref, x_ref, perm_ref, w_ref, g_ref, x1_ref, u2_ref,
                  *, nb, bt):
    def gated(rs):
        f32 = lambda ref: ref[rs, :].astype(F32)
        merged = _sigmoid(f32(ga_ref)) * f32(ya_ref) + _sigmoid(f32(gb_ref)) * f32(yb_ref)
        return merged.astype(BF16)

    if bt == 1:
        x1 = x_ref[...] + jnp.dot(gated(slice(None)), w_ref[...], preferred_element_type=F32)
        x1_ref[...] = x1
        u2_ref[...] = _rmsnorm(x1, g_ref[...]).astype(BF16)
        return
    for s in range(bt // PERM_T):
        ts = slice(s * PERM_T, (s + 1) * PERM_T)
        mb = jnp.dot(perm_ref[...], gated(slice(s * nb * PERM_T, (s + 1) * nb * PERM_T)),
                     preferred_element_type=F32).astype(BF16)
        delta = jnp.dot(mb, w_ref[...], preferred_element_type=F32)
        for b in range(nb):
            x1 = x_ref[b, ts, :] + delta[b * PERM_T:(b + 1) * PERM_T]
            x1_ref[b, ts, :] = x1
            u2_ref[b, ts, :] = _rmsnorm(x1, g_ref[...]).astype(BF16)


def _merge(ya, yb, zr, x, w_out, g, *, nb, seq_len, bt):
    assert bt == 1 or bt % PERM_T == 0
    bm = nb * bt
    row = lambda col: pl.BlockSpec((bm, D_MODEL), lambda i: (i, col))
    perm = _perm_matrix(nb, False) if bt > 1 else jnp.zeros((SUBLANES, LANES), BF16)
    tile = _row_tile_spec(nb, bt, D_MODEL)
    return pl.pallas_call(
        functools.partial(_merge_kernel, nb=nb, bt=bt),
        grid=(seq_len // bt,),
        in_specs=[row(0), row(0), row(0), row(1), tile, _const_spec(perm.shape),
                  _const_spec((D_MODEL, D_MODEL)), _const_spec((1, D_MODEL))],
        out_specs=(tile, tile),
        out_shape=(jax.ShapeDtypeStruct(x.shape, F32), jax.ShapeDtypeStruct(x.shape, BF16)),
        compiler_params=_params(("parallel",)),
        name="merge",
    )(ya, yb, zr, zr, x, perm, w_out, g)


def _ffn_kernel(u_ref, x1_ref, ue_ref, x1e_ref, wg_ref, wu_ref, wd_ref, g_ref, y_ref, ye_ref,
                acc_scr, acce_scr):
    i, f = pl.program_id(0), pl.program_id(1)
    first, last = f == 0, f == pl.num_programs(1) - 1
    with_extra = i == 0
    bm = u_ref.shape[0]

    def down_proj(u):
        hg = jnp.dot(u, wg_ref[...], preferred_element_type=F32)
        hu = jnp.dot(u, wu_ref[...], preferred_element_type=F32)
        hid = (hg * _sigmoid(hg) * hu).astype(BF16)
        return jnp.dot(hid, wd_ref[...], preferred_element_type=F32)

    @pl.when(first)
    def _():
        acc_scr[...] = x1_ref[...]

    @pl.when(first & with_extra)
    def _():
        acce_scr[...] = x1e_ref[...]

    @pl.when(with_extra)
    def _():
        down = down_proj(jnp.concatenate([u_ref[...], ue_ref[...]], axis=0))
        acc_scr[...] += down[:bm]
        acce_scr[...] += down[bm:]

    @pl.when(jnp.logical_not(with_extra))
    def _():
        acc_scr[...] += down_proj(u_ref[...])

    @pl.when(last)
    def _():
        y_ref[...] = _rmsnorm(acc_scr[...], g_ref[...])

    @pl.when(last & with_extra)
    def _():
        ye_ref[...] = _rmsnorm(acce_scr[...], g_ref[...])


def _ffn(u2, x1, u2_extra, x1_extra, wg, wu, wd, g, bm, bf):
    m, me = x1.shape[0], x1_extra.shape[0]
    row = pl.BlockSpec((bm, D_MODEL), lambda i, f: (i, 0))
    extra = pl.BlockSpec((me, D_MODEL), lambda i, f: (0, 0))
    return pl.pallas_call(
        _ffn_kernel,
        grid=(m // bm, D_FF // bf),
        in_specs=[row, row, extra, extra,
                  pl.BlockSpec((D_MODEL, bf), lambda i, f: (0, f)),
                  pl.BlockSpec((D_MODEL, bf), lambda i, f: (0, f)),
                  pl.BlockSpec((bf, D_MODEL), lambda i, f: (f, 0)),
                  pl.BlockSpec((1, D_MODEL), lambda i, f: (0, 0))],
        out_specs=(row, extra),
        out_shape=(jax.ShapeDtypeStruct((m, D_MODEL), F32), jax.ShapeDtypeStruct((me, D_MODEL), F32)),
        scratch_shapes=[pltpu.VMEM((bm, D_MODEL), F32), pltpu.VMEM((me, D_MODEL), F32)],
        compiler_params=_params(("arbitrary", "arbitrary")),
        name="ffn",
    )(u2, x1, u2_extra, x1_extra, wg, wu, wd, g)


def _mixer(x, conv0, h0, s0r, s0i, w, w32, *, bt_in, bt_lru, bt_s5, bt_merge, first_pos_is_zero):
    nb, seq_len = (x.shape[0], 1) if x.ndim == 2 else x.shape[:2]
    seq = dict(nb=nb, seq_len=seq_len)
    w = dict(w)
    take = lambda *names: tuple(w32[n] for n in names) if w32 else ()
    keep = lambda names, arrs: w.update(zip(names, arrs)) if w32 else None

    zl, zr, *cast = _inproj(x, w['norm_mix_g'], w['w_in'], bt=bt_in, cast=take('proj'), **seq)
    keep(('proj',), cast)
    ya, conv_n, h_n, cast = _lru(zl, conv0, h0, w['cw'], w['cb'], w['wax'], w['bax'], w['lam'],
                                 w['proj'], bt=bt_lru, first_pos_is_zero=first_pos_is_zero,
                                 cast=take('w_out', 'wv', 'wg'), **seq)
    keep(('w_out', 'wv', 'wg'), cast)
    yb, sr, si, cast = _s5(zr, s0r, s0i, w['ca_re'], w['ca_im'], w['wb'], w['wc'], w['d'], w['wv'],
                           w['wg'], bt=bt_s5, cast=take('ffn_wg', 'ffn_wu', 'ffn_wd'), **seq)
    keep(('ffn_wg', 'ffn_wu', 'ffn_wd'), cast)
    x1, u2 = _merge(ya, yb, zr, x, w['w_out'], w['norm_ffn_g'], bt=bt_merge, **seq)
    m = nb * seq_len
    return (x1.reshape(m, D_MODEL), u2.reshape(m, D_MODEL), conv_n, h_n, sr, si), w


def kernel(x_prompt, x_sample, state_lru_conv, state_lru_h, state_s5_re, state_s5_im, norm_mix_g, w_in, lru_conv_w, lru_conv_b, lru_wa, lru_ba, lru_wx, lru_bx, lru_lambda, lru_proj, s5_lambda_re, s5_lambda_im, s5_log_dt, s5_b_re, s5_b_im, s5_c_re, s5_c_im, s5_d, s5_glu_wv, s5_glu_wg, w_out, norm_ffn_g, ffn_w_gate, ffn_w_up, ffn_w_down, norm_final_g):
    batch, seq, _ = x_prompt.shape
    dec_batch = x_sample.shape[0]
    l = 0
    ca_re, ca_im, wb, wc = _s5_prepare(s5_lambda_re[l], s5_lambda_im[l], s5_log_dt[l],
                                       s5_b_re[l], s5_b_im[l], s5_c_re[l], s5_c_im[l])
    w = dict(
        norm_mix_g=norm_mix_g[l][None, :], w_in=w_in[l].astype(BF16),
        cw=lru_conv_w[l], cb=lru_conv_b[l][None, :],
        wax=jnp.concatenate([lru_wa[l], lru_wx[l]], axis=-1).astype(BF16),
        bax=jnp.concatenate([lru_ba[l], lru_bx[l]], axis=-1)[:, None, :],
        lam=lru_lambda[l][None, :],
        ca_re=ca_re, ca_im=ca_im, wb=wb, wc=wc, d=s5_d[l].reshape(1, D_S5),
        norm_ffn_g=norm_ffn_g[l][None, :], norm_final_g=norm_final_g[None, :],
    )
    w32 = dict(proj=lru_proj[l], w_out=w_out[l], wv=s5_glu_wv[l], wg=s5_glu_wg[l],
               ffn_wg=ffn_w_gate[l], ffn_wu=ffn_w_up[l], ffn_wd=ffn_w_down[l])
    zeros = lambda *s: jnp.zeros(s, F32)
    (x1_p, u2_p, conv_p, h_p, sr_p, si_p), w = _mixer(
        x_prompt, zeros(batch, CONV_W - 1, D_LRU), zeros(batch, D_LRU),
        zeros(batch, S5_N), zeros(batch, S5_N), w, w32,
        bt_in=256, bt_lru=128, bt_s5=64, bt_merge=128, first_pos_is_zero=True)
    (x1_s, u2_s, conv_s, h_s, sr_s, si_s), _ = _mixer(
        x_sample.reshape(dec_batch, D_MODEL), state_lru_conv[l], state_lru_h[l],
        state_s5_re[l].reshape(dec_batch, S5_N), state_s5_im[l].reshape(dec_batch, S5_N), w, None,
        bt_in=1, bt_lru=1, bt_s5=1, bt_merge=1, first_pos_is_zero=False)
    yp, ys = _ffn(u2_p, x1_p, u2_s, x1_s, w['ffn_wg'], w['ffn_wu'], w['ffn_wd'], w['norm_final_g'],
                  bm=512, bf=512)
    st = lambda a, n: a.reshape(1, n, S5_G, S5_P)
    return (yp.reshape(batch, seq, D_MODEL), ys.reshape(dec_batch, 1, D_MODEL),
            conv_p[None], h_p[None], st(sr_p, batch), st(si_p, batch),
            conv_s[None], h_s[None], st(sr_s, dec_batch), st(si_s, dec_batch))
```

```python
import functools
import math

import jax
import jax.numpy as jnp
import numpy as np
from jax import lax
from jax.experimental import pallas as pl
from jax.experimental.pallas import tpu as pltpu

F32 = jnp.float32
BF16 = jnp.bfloat16

D_MODEL = 2048
D_LRU = D_MODEL
H_LRU = 8
LRU_BW = D_LRU // H_LRU
LRU_C = 8.0
CONV_W = 4
D_S5 = D_MODEL // 2
S5_H = 16
S5_G = D_S5 // S5_H
S5_P = 64
S5_N = S5_G * S5_P
D_FF = 5632
IN_WIDTH = D_LRU + D_S5 + 2 * D_MODEL
EPS = 1e-6

S5_SLAB_G = 8
S5_SLABS = S5_G // S5_SLAB_G
S5_SLAB_IN = S5_SLAB_G * S5_H
S5_SLAB_ST = S5_SLAB_G * S5_P

LANES = 128
SUBLANES = 8
HALF = SUBLANES // 2
VMEM_LIMIT_BYTES = 56 * 1024 * 1024


def _params(sem):
    return pltpu.CompilerParams(dimension_semantics=sem, vmem_limit_bytes=VMEM_LIMIT_BYTES)


def _const_spec(shape):
    nd = len(shape)
    return pl.BlockSpec(shape, lambda *_: (0,) * nd, pipeline_mode=pl.Buffered(1))


def _side_cast_specs(arrs, n_steps, step_of):
    specs, shapes = [], []
    for a in arrs:
        rows = a.shape[0] // n_steps
        assert rows * n_steps == a.shape[0] and rows % (2 * SUBLANES) == 0
        specs.append(pl.BlockSpec((rows, a.shape[1]), lambda *g: (step_of(*g), 0)))
        shapes.append(jax.ShapeDtypeStruct(a.shape, BF16))
    return specs, shapes


def _split_refs(refs, n_in, n_out, n_cast):
    cuts = [n_in, n_in + n_cast, n_in + n_cast + n_out, n_in + 2 * n_cast + n_out]
    parts = [refs[a:b] for a, b in zip([0] + cuts, cuts + [len(refs)])]
    return parts


def _rmsnorm(x, g):
    ms = jnp.mean(x * x, axis=-1, keepdims=True)
    return x * lax.rsqrt(ms + EPS) * g


LOG2E = 1.4426950408889634


def _sigmoid(x):
    return 1.0 / (1.0 + jnp.exp2(x * (-LOG2E)))


def _lo_rows(shape):
    return lax.broadcasted_iota(jnp.int32, shape, 0) % SUBLANES < HALF


def _s5prep_kernel(lre_ref, lim_ref, ldt_ref, br_ref, bi_ref,
                   bbr_ref, bbi_ref, abbr_ref, abbi_ref, abr_ref, abi_ref, a2r_ref, a2i_ref):
    lre = lre_ref[...]
    lim = lim_ref[...]
    dt = jnp.exp(ldt_ref[...])
    mag = jnp.exp(lre * dt)
    ab_re = mag * jnp.cos(lim * dt)
    ab_im = mag * jnp.sin(lim * dt)
    e_re = ab_re - 1.0
    e_im = ab_im
    den = lre * lre + lim * lim
    co_re = (e_re * lre + e_im * lim) / den
    co_im = (e_im * lre - e_re * lim) / den
    br = br_ref[...]
    bi = bi_ref[...]
    bb_re = co_re * br - co_im * bi
    bb_im = co_re * bi + co_im * br
    bbr_ref[...] = bb_re
    bbi_ref[...] = bb_im
    abbr_ref[...] = ab_re * bb_re - ab_im * bb_im
    abbi_ref[...] = ab_re * bb_im + ab_im * bb_re
    abr_ref[...] = ab_re
    abi_ref[...] = ab_im
    a2r_ref[...] = ab_re * ab_re - ab_im * ab_im
    a2i_ref[...] = 2.0 * (ab_re * ab_im)


def _s5_prepare(lam_re, lam_im, log_dt, b_re, b_im, c_re, c_im):
    full = (S5_G, S5_P, S5_H)
    flat = (S5_G * S5_P * S5_H // LANES, LANES)
    bc = lambda a: jnp.broadcast_to(a, full).reshape(flat)
    args = (bc(lam_re[:, :, None]), bc(lam_im[:, :, None]), bc(log_dt[:, None, None]),
            b_re.reshape(flat), b_im.reshape(flat))
    sds = jax.ShapeDtypeStruct(flat, F32)
    bbr, bbi, abbr, abbi, abr, abi, a2r, a2i = pl.pallas_call(
        _s5prep_kernel, out_shape=(sds,) * 8, name="s5prep")(*args)
    per_state = lambda a: a.reshape(full)[:, :, 0].reshape(1, S5_N)
    coeff = lambda a1, a2: jnp.concatenate(
        [jnp.broadcast_to(per_state(a1), (HALF, S5_N)), jnp.broadcast_to(per_state(a2), (HALF, S5_N))], 0)
    ca_re, ca_im = coeff(abr, a2r), coeff(abi, a2i)
    eye = np.eye(S5_SLAB_G, dtype=np.float32)

    def slab_b(re, im):
        b = jnp.stack([re.reshape(full), im.reshape(full)], 0)
        b = b.reshape(2, S5_SLABS, S5_SLAB_G, S5_P, S5_H)
        return jnp.einsum('cjgph,gk->jghckp', b, eye).reshape(S5_SLABS, S5_SLAB_IN, 2 * S5_SLAB_ST)

    wb = jnp.concatenate([slab_b(bbr, bbi), slab_b(abbr, abbi)], axis=1)
    cc = jnp.stack([c_re, -c_im], 0).reshape(2, S5_SLABS, S5_SLAB_G, S5_H, S5_P)
    wc = jnp.einsum('cjghp,gk->jcgpkh', cc, eye).reshape(S5_SLABS, 2 * S5_SLAB_ST, S5_SLAB_IN)
    return ca_re, ca_im, wb.astype(BF16), wc.astype(BF16)


PERM_T = 64


def _perm_matrix(nb, to_time_major):
    n = nb * PERM_T
    tm = np.arange(n)
    sm = (tm % nb) * PERM_T + tm // nb
    p = np.zeros((n, n), np.float32)
    p[tm, sm] = 1.0
    return jnp.asarray(p if to_time_major else p.T, dtype=BF16)


def _row_tile_spec(nb, bt, width):
    if bt == 1:
        return pl.BlockSpec((nb, width), lambda *g: (0, 0))
    return pl.BlockSpec((nb, bt, width), lambda *g: (0, g[0], 0))


N_LRU_BLOCKS = 2
INPROJ_BN = 1024
ZR_WIDTH = IN_WIDTH - D_LRU


def _zr_block(j):
    return jnp.where(j <= N_LRU_BLOCKS, (ZR_WIDTH - D_S5) // INPROJ_BN, j - N_LRU_BLOCKS - 1)


def _inproj_kernel(*refs, nb, bt, n_cast, emit_w):
    (x_ref, g_ref, perm_ref, w_ref), cast_src, outs, cast_dst, (u_scr,) = _split_refs(
        refs, 4, 2 + int(emit_w), n_cast)
    zl_ref, zr_ref = outs[:2]
    j = pl.program_id(1)

    @pl.when(j == 0)
    def _():
        for src, dst in zip(cast_src, cast_dst):
            dst[...] = src[...].astype(BF16)
        if bt == 1:
            u_scr[...] = _rmsnorm(x_ref[...], g_ref[...]).astype(BF16)
        else:
            for s in range(bt // PERM_T):
                ts = slice(s * PERM_T, (s + 1) * PERM_T)
                xin = jnp.concatenate([x_ref[b, ts, :] for b in range(nb)], axis=0)
                u = _rmsnorm(xin, g_ref[...]).astype(BF16)
                u_scr[s * nb * PERM_T:(s + 1) * nb * PERM_T, :] = jnp.dot(
                    perm_ref[...], u, preferred_element_type=F32).astype(BF16)

    w = w_ref[...]
    if emit_w:
        w = w.astype(BF16)
        outs[2][...] = w
    acc = jnp.dot(u_scr[...], w, preferred_element_type=F32)

    @pl.when(j < N_LRU_BLOCKS)
    def _():
        zl_ref[...] = acc

    @pl.when(j >= N_LRU_BLOCKS)
    def _():
        zr_ref[...] = acc.astype(BF16)


def _inproj(x, g, w, *, nb, seq_len, bt, cast=()):
    assert bt == 1 or bt % PERM_T == 0
    m, bm, bn = nb * seq_len, nb * bt, INPROJ_BN
    emit_w = w.dtype == F32
    assert not emit_w or m == bm
    w_spec = pl.BlockSpec((D_MODEL, bn), lambda i, j: (0, j))
    perm = _perm_matrix(nb, True) if bt > 1 else jnp.zeros((SUBLANES, LANES), BF16)
    cast_specs, cast_shapes = _side_cast_specs(cast, m // bm, lambda i, j: i)
    return pl.pallas_call(
        functools.partial(_inproj_kernel, nb=nb, bt=bt, n_cast=len(cast), emit_w=emit_w),
        grid=(m // bm, IN_WIDTH // bn),
        in_specs=[_row_tile_spec(nb, bt, D_MODEL),
                  pl.BlockSpec((1, D_MODEL), lambda i, j: (0, 0)),
                  _const_spec(perm.shape),
                  w_spec] + cast_specs,
        out_specs=[pl.BlockSpec((bm, bn), lambda i, j: (i, jnp.minimum(j, N_LRU_BLOCKS - 1))),
                   pl.BlockSpec((bm, bn), lambda i, j: (i, _zr_block(j)))]
        + [w_spec] * int(emit_w) + cast_specs,
        out_shape=[jax.ShapeDtypeStruct((m, D_LRU), F32), jax.ShapeDtypeStruct((m, ZR_WIDTH), BF16)]
        + [jax.ShapeDtypeStruct(w.shape, BF16)] * int(emit_w) + cast_shapes,
        scratch_shapes=[pltpu.VMEM((bm, D_MODEL), BF16)],
        compiler_params=_params(("parallel", "arbitrary")),
        name="inproj",
    )(x, g, perm, w, *cast)


def _lru_kernel(*refs, nb, bt, n_tiles, first_pos_is_zero, n_cast):
    ((x_ref, conv0_ref, h0_ref, cw_ref, cb_ref, wax_ref, bax_ref, lam_ref, proj_ref),
     cast_src, (ya_ref, conv_ref, ht_ref), cast_dst,
     (xe_scr, a_scr, b_scr, hc_scr, hq_scr)) = _split_refs(refs, 9, 3, n_cast)
    c = pl.program_id(0)
    rows = nb * bt
    pad = 2 * SUBLANES
    pipelined = bt > 1

    @pl.when(c == 0)
    def _():
        conv_ref[...] = conv0_ref[...]
        ht_ref[...] = h0_ref[...]
        if pipelined:
            xe_scr[:pad - 3 * nb, :] = jnp.zeros((pad - 3 * nb, D_LRU), F32)
            hq_scr[...] = jnp.zeros(hq_scr.shape, BF16)

    nlam = -lam_ref[...]
    sp = jnp.maximum(nlam, 0.0) + jnp.log1p(jnp.exp(-jnp.abs(nlam)))
    k_tanh = LRU_C * sp
    k_exp2 = (-LRU_C * LOG2E) * sp

    if bt > 1:
        for k in range(CONV_W - 1):
            xe_scr[pad - (3 - k) * nb:pad - (2 - k) * nb, :] = conv_ref[:, k, :]
        xe_scr[pad:, :] = x_ref[...]
        is_first = (lax.broadcasted_iota(jnp.int32, (rows, LRU_BW), 0) < nb) & (c == 0)

    for hb in range(H_LRU):
        cs = slice(hb * LRU_BW, (hb + 1) * LRU_BW)
        if pipelined:
            ya = jnp.dot(hq_scr[...], proj_ref[:, cs], preferred_element_type=F32)
            ya_ref[:, cs] = ya.astype(BF16)
        if hb < n_cast:
            cast_dst[hb][...] = cast_src[hb][...].astype(BF16)
        if bt > 1:
            xs = pltpu.roll(xe_scr[:, cs], nb, axis=0)
            xc = (cb_ref[:, cs] + xs[pad - 2 * nb:pad + rows - 2 * nb] * cw_ref[0:1, cs]
                  + xe_scr[pad - 2 * nb:pad + rows - 2 * nb, cs] * cw_ref[1:2, cs]
                  + xs[pad:] * cw_ref[2:3, cs]
                  + xe_scr[pad:, cs] * cw_ref[3:4, cs])
        else:
            xc = (cb_ref[:, cs] + conv_ref[:, 0, cs] * cw_ref[0:1, cs]
                  + conv_ref[:, 1, cs] * cw_ref[1:2, cs] + conv_ref[:, 2, cs] * cw_ref[2:3, cs]
                  + x_ref[:, cs] * cw_ref[3:4, cs])
        gates = jnp.dot(xc.astype(BF16), wax_ref[hb], preferred_element_type=F32) + bax_ref[hb]
        r = _sigmoid(gates[:, :LRU_BW])
        i = _sigmoid(gates[:, LRU_BW:])
        a = jnp.exp2(r * k_exp2[:, cs])
        m2 = jnp.tanh(r * k_tanh[:, cs]) * (1.0 + a * a)
        mult = jnp.where(m2 > 0.0, m2 * lax.rsqrt(m2), 0.0)
        if first_pos_is_zero:
            mult = jnp.where(is_first, 1.0, mult)
        a_scr[:, cs] = a
        b_scr[:, cs] = mult * (i * xc)

    if not pipelined:
        conv_ref[:, 0, :] = conv_ref[:, 1, :]
        conv_ref[:, 1, :] = conv_ref[:, 2, :]
        conv_ref[:, 2, :] = x_ref[...]
        h = a_scr[...] * ht_ref[...] + b_scr[...]
        ht_ref[...] = h
        ya = jnp.dot(h.astype(BF16), proj_ref[...], preferred_element_type=F32)
        ya_ref[...] = ya.astype(BF16)
        return

    @pl.when(c < n_tiles)
    def _():
        for k in range(CONV_W - 1):
            conv_ref[:, k, :] = x_ref[rows - (3 - k) * nb:rows - (2 - k) * nb, :]

        n_cb = D_LRU // LANES
        lo = _lo_rows((SUBLANES, LANES))
        hc_scr[:HALF, :] = ht_ref[...]
        hc_scr[HALF:, :] = ht_ref[...]

        def pair(k, hs):
            rs = pl.ds(pl.multiple_of(k * SUBLANES, SUBLANES), SUBLANES)
            out = []
            for cb in range(n_cb):
                cl = slice(cb * LANES, (cb + 1) * LANES)
                av, bv = a_scr[rs, cl], b_scr[rs, cl]
                first = av * hs[cb] + bv
                second = av * pltpu.roll(first, HALF, axis=0) + bv
                h = jnp.where(lo, first, second)
                b_scr[rs, cl] = h
                out.append(pltpu.roll(h, HALF, axis=0))
            return tuple(out)

        hs = tuple(hc_scr[:, cb * LANES:(cb + 1) * LANES] for cb in range(n_cb))
        hs = lax.fori_loop(0, rows // SUBLANES, pair, hs)
        for cb in range(n_cb):
            hc_scr[:, cb * LANES:(cb + 1) * LANES] = hs[cb]
        ht_ref[...] = hc_scr[:HALF, :]
        hq_scr[...] = b_scr[...].astype(BF16)


def _lru(z, conv0, h0, cw, cb, wax, bax, lam, proj, *, nb, seq_len, bt, first_pos_is_zero, cast=()):
    assert bt == 1 or (nb == HALF and bt % 2 == 0 and bt >= 4)
    assert len(cast) <= H_LRU
    rows, n_tiles = nb * bt, seq_len // bt
    pipelined = bt > 1
    st = pl.BlockSpec((nb, D_LRU), lambda c: (0, 0))
    st_sds = jax.ShapeDtypeStruct((nb, D_LRU), F32)
    cst = pl.BlockSpec((nb, CONV_W - 1, D_LRU), lambda c: (0, 0, 0))
    cst_sds = jax.ShapeDtypeStruct((nb, CONV_W - 1, D_LRU), F32)
    x_tile = pl.BlockSpec((rows, D_LRU), lambda c: (jnp.minimum(c, n_tiles - 1), 0))
    ya_tile = pl.BlockSpec((rows, D_LRU), lambda c: (jnp.maximum(c - int(pipelined), 0), 0))
    kern = functools.partial(_lru_kernel, nb=nb, bt=bt, n_tiles=n_tiles,
                             first_pos_is_zero=first_pos_is_zero, n_cast=len(cast))
    cast_specs, cast_shapes = _side_cast_specs(cast, n_tiles, lambda c: jnp.minimum(c, n_tiles - 1))
    small = (SUBLANES, LANES)
    ya, conv, ht, *cast_out = pl.pallas_call(
        kern,
        grid=(n_tiles + int(pipelined),),
        in_specs=[x_tile, cst, st,
                  _const_spec((CONV_W, D_LRU)), _const_spec((1, D_LRU)),
                  _const_spec((H_LRU, LRU_BW, 2 * LRU_BW)), _const_spec((H_LRU, 1, 2 * LRU_BW)),
                  _const_spec((1, D_LRU)), _const_spec((D_LRU, D_MODEL))] + cast_specs,
        out_specs=[ya_tile, cst, st] + cast_specs,
        out_shape=[jax.ShapeDtypeStruct((nb * seq_len, D_LRU), BF16), cst_sds, st_sds] + cast_shapes,
        scratch_shapes=[pltpu.VMEM((rows + 2 * SUBLANES, D_LRU) if pipelined else small, F32),
                        pltpu.VMEM((rows, D_LRU), F32), pltpu.VMEM((rows, D_LRU), F32),
                        pltpu.VMEM((SUBLANES, D_LRU), F32),
                        pltpu.VMEM((rows, D_LRU) if pipelined else (2 * SUBLANES, LANES), BF16)],
        compiler_params=_params(("arbitrary",)),
        name="lru",
    )(z, conv0, h0, cw, cb, wax, bax, lam, proj, *cast)
    return ya, conv, ht, cast_out


def _gelu_tanh(x):
    c = math.sqrt(2.0 / math.pi)
    return 0.5 * x * (1.0 + jnp.tanh(c * (x + 0.044715 * (x * x * x))))


S5_GROUPS = 4
S5_GROUP_SLABS = S5_SLABS // S5_GROUPS
S5_GROUP_COLS = S5_N // S5_GROUPS
GLU_GROUP_COLS = D_MODEL // S5_GROUPS


def _s5_b_proj(ub, up, wb_ref, xr_scr, xi_scr, j):
    js = slice(j * S5_SLAB_IN, (j + 1) * S5_SLAB_IN)
    if up is None:
        w = jnp.dot(ub[:, js], wb_ref[j, :S5_SLAB_IN, :], preferred_element_type=F32)
    else:
        w = jnp.dot(jnp.concatenate([ub[:, js], up[:, js]], axis=1), wb_ref[j],
                    preferred_element_type=F32)
    xr_scr[:, j * S5_SLAB_ST:(j + 1) * S5_SLAB_ST] = w[:, :S5_SLAB_ST]
    xi_scr[:, j * S5_SLAB_ST:(j + 1) * S5_SLAB_ST] = w[:, S5_SLAB_ST:]


def _s5_c_proj(xr_scr, xi_scr, wc_ref, y_scr, j):
    ss = slice(j * S5_SLAB_ST, (j + 1) * S5_SLAB_ST)
    xs = jnp.concatenate([xr_scr[:, ss].astype(BF16), xi_scr[:, ss].astype(BF16)], axis=1)
    y_scr[:, j * S5_SLAB_IN:(j + 1) * S5_SLAB_IN] = jnp.dot(xs, wc_ref[j], preferred_element_type=F32)


def _s5_kernel(*refs, nb, bt, n_tiles, n_cast):
    ((u_ref, s0r_ref, s0i_ref, car_ref, cai_ref, wb_ref, wc_ref, d_ref, wv_ref, wg_ref), cast_src,
     (yb_ref, sr_ref, si_ref), cast_dst,
     (xr_scr, xi_scr, pr_scr, pi_scr, y_scr, v_scr)) = _split_refs(refs, 10, 3, n_cast)
    c = pl.program_id(0)
    rows = nb * bt
    ub = u_ref[...]
    u = ub.astype(F32)

    if bt == 1:
        for j in range(S5_SLABS):
            _s5_b_proj(ub, None, wb_ref, xr_scr, xi_scr, j)
        ar, ai = car_ref[0:1, :], cai_ref[0:1, :]
        pr, pi = s0r_ref[...], s0i_ref[...]
        nr = ar * pr - ai * pi + xr_scr[...]
        ni = ar * pi + ai * pr + xi_scr[...]
        xr_scr[...] = nr
        xi_scr[...] = ni
        sr_ref[...] = nr
        si_ref[...] = ni
        for j in range(S5_SLABS):
            _s5_c_proj(xr_scr, xi_scr, wc_ref, y_scr, j)
        v = _gelu_tanh(y_scr[...] + d_ref[...] * u).astype(BF16)
        yv = jnp.dot(v, wv_ref[...], preferred_element_type=F32)
        yg = jnp.dot(v, wg_ref[...], preferred_element_type=F32)
        yb_ref[...] = (yv * _sigmoid(yg)).astype(BF16)
        return

    @pl.when(c == 0)
    def _():
        pr_scr[:HALF, :] = s0r_ref[...]
        pr_scr[HALF:, :] = s0r_ref[...]
        pi_scr[:HALF, :] = s0i_ref[...]
        pi_scr[HALF:, :] = s0i_ref[...]
        v_scr[...] = jnp.zeros(v_scr.shape, BF16)

    up = jnp.where(_lo_rows(u.shape), 0.0, pltpu.roll(u, HALF, axis=0)).astype(BF16)
    lo = _lo_rows((SUBLANES, LANES))
    v_prev = v_scr[...]
    for grp in range(S5_GROUPS):
        gs = slice(grp * GLU_GROUP_COLS, (grp + 1) * GLU_GROUP_COLS)
        yv = jnp.dot(v_prev, wv_ref[:, gs], preferred_element_type=F32)
        yg = jnp.dot(v_prev, wg_ref[:, gs], preferred_element_type=F32)
        yb_ref[:, gs] = (yv * _sigmoid(yg)).astype(BF16)
        if grp < n_cast:
            cast_dst[grp][...] = cast_src[grp][...].astype(BF16)

        slabs = range(grp * S5_GROUP_SLABS, (grp + 1) * S5_GROUP_SLABS)
        for j in slabs:
            _s5_b_proj(ub, up, wb_ref, xr_scr, xi_scr, j)
        cols = [slice(grp * S5_GROUP_COLS + cb * LANES, grp * S5_GROUP_COLS + (cb + 1) * LANES)
                for cb in range(S5_GROUP_COLS // LANES)]
        carry = [(pr_scr[:, cl], pi_scr[:, cl]) for cl in cols]
        for k in range(rows // SUBLANES):
            rs = slice(k * SUBLANES, (k + 1) * SUBLANES)
            for cb, cl in enumerate(cols):
                pr, pi = carry[cb]
                ar, ai = car_ref[:, cl], cai_ref[:, cl]
                nr = ar * pr - ai * pi + xr_scr[rs, cl]
                ni = ar * pi + ai * pr + xi_scr[rs, cl]
                xr_scr[rs, cl] = nr
                xi_scr[rs, cl] = ni
                carry[cb] = (jnp.where(lo, pltpu.roll(nr, HALF, axis=0), nr),
                             jnp.where(lo, pltpu.roll(ni, HALF, axis=0), ni))
        for cb, cl in enumerate(cols):
            pr_scr[:, cl], pi_scr[:, cl] = carry[cb]
        for j in slabs:
            _s5_c_proj(xr_scr, xi_scr, wc_ref, y_scr, j)
    v_scr[...] = _gelu_tanh(y_scr[...] + d_ref[...] * u).astype(BF16)

    @pl.when(c == n_tiles - 1)
    def _():
        sr_ref[...] = pr_scr[:HALF, :]
        si_ref[...] = pi_scr[:HALF, :]


def _s5(z, s0r, s0i, ca_re, ca_im, wb, wc, d, wv, wg, *, nb, seq_len, bt, cast=()):
    assert bt == 1 or (nb == HALF and bt % 2 == 0)
    assert len(cast) <= S5_GROUPS
    rows, n_tiles = nb * bt, seq_len // bt
    pipelined = bt > 1
    st = pl.BlockSpec((nb, S5_N), lambda c: (0, 0))
    st_sds = jax.ShapeDtypeStruct((nb, S5_N), F32)
    x_s5_block = (ZR_WIDTH - D_S5) // D_S5
    cast_specs, cast_shapes = _side_cast_specs(cast, n_tiles, lambda c: jnp.minimum(c, n_tiles - 1))
    yb, sr, si, *cast_out = pl.pallas_call(
        functools.partial(_s5_kernel, nb=nb, bt=bt, n_tiles=n_tiles, n_cast=len(cast)),
        grid=(n_tiles + int(pipelined),),
        in_specs=[pl.BlockSpec((rows, D_S5), lambda c: (jnp.minimum(c, n_tiles - 1), x_s5_block)),
                  st, st, _const_spec((SUBLANES, S5_N)), _const_spec((SUBLANES, S5_N)),
                  _const_spec((S5_SLABS, 2 * S5_SLAB_IN, 2 * S5_SLAB_ST)),
                  _const_spec((S5_SLABS, 2 * S5_SLAB_ST, S5_SLAB_IN)),
                  _const_spec((1, D_S5)), _const_spec((D_S5, D_MODEL)),
                  _const_spec((D_S5, D_MODEL))] + cast_specs,
        out_specs=[pl.BlockSpec((rows, D_MODEL), lambda c: (jnp.maximum(c - int(pipelined), 0), 0)),
                   st, st] + cast_specs,
        out_shape=[jax.ShapeDtypeStruct((nb * seq_len, D_MODEL), BF16), st_sds, st_sds] + cast_shapes,
        scratch_shapes=[pltpu.VMEM((rows, S5_N), F32), pltpu.VMEM((rows, S5_N), F32),
                        pltpu.VMEM((SUBLANES, S5_N), F32), pltpu.VMEM((SUBLANES, S5_N), F32),
                        pltpu.VMEM((rows, D_S5), F32), pltpu.VMEM((rows, D_S5), BF16)],
        compiler_params=_params(("arbitrary",)),
        name="s5",
    )(z, s0r, s0i, ca_re, ca_im, wb, wc, d, wv, wg, *cast)
    return yb, sr, si, cast_out


def _merge_kernel(ya_ref, yb_ref, ga_ref, gb_ref, x_ref, perm_ref, w_ref, g_ref, x1_ref, u2_ref,
                  *, nb, bt):
    def gated(rs):
        f32 = lambda ref: ref[rs, :].astype(F32)
        merged = _sigmoid(f32(ga_ref)) * f32(ya_ref) + _sigmoid(f32(gb_ref)) * f32(yb_ref)
        return merged.astype(BF16)

    if bt == 1:
        x1 = x_ref[...] + jnp.dot(gated(slice(None)), w_ref[...], preferred_element_type=F32)
        x1_ref[...] = x1
        u2_ref[...] = _rmsnorm(x1, g_ref[...]).astype(BF16)
        return
    n_sub = bt // PERM_T
    sub_rows = lambda s: slice(s * nb * PERM_T, (s + 1) * nb * PERM_T)

    def finish(s, delta):
        ts = slice(s * PERM_T, (s + 1) * PERM_T)
        for b in range(nb):
            x1 = x_ref[b, ts, :] + delta[b * PERM_T:(b + 1) * PERM_T]
            x1_ref[b, ts, :] = x1
            u2_ref[b, ts, :] = _rmsnorm(x1, g_ref[...]).astype(BF16)

    merged = gated(sub_rows(0))
    pending = None
    for s in range(n_sub):
        mb = jnp.dot(perm_ref[...], merged, preferred_element_type=F32).astype(BF16)
        delta = jnp.dot(mb, w_ref[...], preferred_element_type=F32)
        if s + 1 < n_sub:
            merged = gated(sub_rows(s + 1))
        if pending is not None:
            finish(*pending)
        pending = (s, delta)
    finish(*pending)


def _merge(ya, yb, zr, x, w_out, g, *, nb, seq_len, bt):
    assert bt == 1 or bt % PERM_T == 0
    bm = nb * bt
    row = lambda col: pl.BlockSpec((bm, D_MODEL), lambda i: (i, col))
    perm = _perm_matrix(nb, False) if bt > 1 else jnp.zeros((SUBLANES, LANES), BF16)
    tile = _row_tile_spec(nb, bt, D_MODEL)
    return pl.pallas_call(
        functools.partial(_merge_kernel, nb=nb, bt=bt),
        grid=(seq_len // bt,),
        in_specs=[row(0), row(0), row(0), row(1), tile, _const_spec(perm.shape),
                  _const_spec((D_MODEL, D_MODEL)), _const_spec((1, D_MODEL))],
        out_specs=(tile, tile),
        out_shape=(jax.ShapeDtypeStruct(x.shape, F32), jax.ShapeDtypeStruct(x.shape, BF16)),
        compiler_params=_params(("parallel",)),
        name="merge",
    )(ya, yb, zr, zr, x, perm, w_out, g)


def _ffn_kernel(u_ref, x1_ref, ue_ref, x1e_ref, wg_ref, wu_ref, wd_ref, g_ref, y_ref, ye_ref,
                acc_scr, acce_scr):
    i, f = pl.program_id(0), pl.program_id(1)
    first, last = f == 0, f == pl.num_programs(1) - 1
    with_extra = i == 0
    bm = u_ref.shape[0]

    def down_proj(u):
        hg = jnp.dot(u, wg_ref[...], preferred_element_type=F32)
        hu = jnp.dot(u, wu_ref[...], preferred_element_type=F32)
        hid = (hg * _sigmoid(hg) * hu).astype(BF16)
        return jnp.dot(hid, wd_ref[...], preferred_element_type=F32)

    @pl.when(first)
    def _():
        acc_scr[...] = x1_ref[...]

    @pl.when(first & with_extra)
    def _():
        acce_scr[...] = x1e_ref[...]

    @pl.when(with_extra)
    def _():
        down = down_proj(jnp.concatenate([u_ref[...], ue_ref[...]], axis=0))
        acc_scr[...] += down[:bm]
        acce_scr[...] += down[bm:]

    @pl.when(jnp.logical_not(with_extra))
    def _():
        acc_scr[...] += down_proj(u_ref[...])

    @pl.when(last)
    def _():
        y_ref[...] = _rmsnorm(acc_scr[...], g_ref[...])

    @pl.when(last & with_extra)
    def _():
        ye_ref[...] = _rmsnorm(acce_scr[...], g_ref[...])


def _ffn(u2, x1, u2_extra, x1_extra, wg, wu, wd, g, bm, bf):
    m, me = x1.shape[0], x1_extra.shape[0]
    row = pl.BlockSpec((bm, D_MODEL), lambda i, f: (i, 0))
    extra = pl.BlockSpec((me, D_MODEL), lambda i, f: (0, 0))
    return pl.pallas_call(
        _ffn_kernel,
        grid=(m // bm, D_FF // bf),
        in_specs=[row, row, extra, extra,
                  pl.BlockSpec((D_MODEL, bf), lambda i, f: (0, f)),
                  pl.BlockSpec((D_MODEL, bf), lambda i, f: (0, f)),
                  pl.BlockSpec((bf, D_MODEL), lambda i, f: (f, 0)),
                  pl.BlockSpec((1, D_MODEL), lambda i, f: (0, 0))],
        out_specs=(row, extra),
        out_shape=(jax.ShapeDtypeStruct((m, D_MODEL), F32), jax.ShapeDtypeStruct((me, D_MODEL), F32)),
        scratch_shapes=[pltpu.VMEM((bm, D_MODEL), F32), pltpu.VMEM((me, D_MODEL), F32)],
        compiler_params=_params(("arbitrary", "arbitrary")),
        name="ffn",
    )(u2, x1, u2_extra, x1_extra, wg, wu, wd, g)


def _mixer(x, conv0, h0, s0r, s0i, w, w32, *, bt_in, bt_lru, bt_s5, bt_merge, first_pos_is_zero,
           z=None):
    nb, seq_len = (x.shape[0], 1) if x.ndim == 2 else x.shape[:2]
    seq = dict(nb=nb, seq_len=seq_len)
    w = dict(w)
    take = lambda *names: tuple(w32[n] for n in names) if w32 else ()
    keep = lambda names, arrs: w.update(zip(names, arrs)) if w32 else None

    if z is None:
        zl, zr, *cast = _inproj(x, w['norm_mix_g'], w['w_in'], bt=bt_in, cast=take('proj'), **seq)
        keep(('proj',), cast)
    else:
        zl, zr = z
    ya, conv_n, h_n, cast = _lru(zl, conv0, h0, w['cw'], w['cb'], w['wax'], w['bax'], w['lam'],
                                 w['proj'], bt=bt_lru, first_pos_is_zero=first_pos_is_zero,
                                 cast=take('w_out', 'wv', 'wg'), **seq)
    keep(('w_out', 'wv', 'wg'), cast)
    yb, sr, si, cast = _s5(zr, s0r, s0i, w['ca_re'], w['ca_im'], w['wb'], w['wc'], w['d'], w['wv'],
                           w['wg'], bt=bt_s5, cast=take('ffn_wg', 'ffn_wu', 'ffn_wd'), **seq)
    keep(('ffn_wg', 'ffn_wu', 'ffn_wd'), cast)
    x1, u2 = _merge(ya, yb, zr, x, w['w_out'], w['norm_ffn_g'], bt=bt_merge, **seq)
    m = nb * seq_len
    return (x1.reshape(m, D_MODEL), u2.reshape(m, D_MODEL), conv_n, h_n, sr, si), w


def kernel(x_prompt, x_sample, state_lru_conv, state_lru_h, state_s5_re, state_s5_im, norm_mix_g, w_in, lru_conv_w, lru_conv_b, lru_wa, lru_ba, lru_wx, lru_bx, lru_lambda, lru_proj, s5_lambda_re, s5_lambda_im, s5_log_dt, s5_b_re, s5_b_im, s5_c_re, s5_c_im, s5_d, s5_glu_wv, s5_glu_wg, w_out, norm_ffn_g, ffn_w_gate, ffn_w_up, ffn_w_down, norm_final_g):
    batch, seq, _ = x_prompt.shape
    dec_batch = x_sample.shape[0]
    l = 0
    ca_re, ca_im, wb, wc = _s5_prepare(s5_lambda_re[l], s5_lambda_im[l], s5_log_dt[l],
                                       s5_b_re[l], s5_b_im[l], s5_c_re[l], s5_c_im[l])
    w = dict(
        norm_mix_g=norm_mix_g[l][None, :],
        cw=lru_conv_w[l], cb=lru_conv_b[l][None, :],
        wax=jnp.concatenate([lru_wa[l], lru_wx[l]], axis=-1).astype(BF16),
        bax=jnp.concatenate([lru_ba[l], lru_bx[l]], axis=-1)[:, None, :],
        lam=lru_lambda[l][None, :],
        ca_re=ca_re, ca_im=ca_im, wb=wb, wc=wc, d=s5_d[l].reshape(1, D_S5),
        norm_ffn_g=norm_ffn_g[l][None, :], norm_final_g=norm_final_g[None, :],
    )
    w32 = dict(proj=lru_proj[l], w_out=w_out[l], wv=s5_glu_wv[l], wg=s5_glu_wg[l],
               ffn_wg=ffn_w_gate[l], ffn_wu=ffn_w_up[l], ffn_wd=ffn_w_down[l])
    zeros = lambda *s: jnp.zeros(s, F32)
    xs = x_sample.reshape(dec_batch, D_MODEL)
    zl_s, zr_s, w['w_in'] = _inproj(xs, w['norm_mix_g'], w_in[l], nb=dec_batch, seq_len=1, bt=1)
    (x1_p, u2_p, conv_p, h_p, sr_p, si_p), w = _mixer(
        x_prompt, zeros(batch, CONV_W - 1, D_LRU), zeros(batch, D_LRU),
        zeros(batch, S5_N), zeros(batch, S5_N), w, w32,
        bt_in=256, bt_lru=128, bt_s5=64, bt_merge=128, first_pos_is_zero=True)
    (x1_s, u2_s, conv_s, h_s, sr_s, si_s), _ = _mixer(
        xs, state_lru_conv[l], state_lru_h[l],
        state_s5_re[l].reshape(dec_batch, S5_N), state_s5_im[l].reshape(dec_batch, S5_N), w, None,
        bt_in=1, bt_lru=1, bt_s5=1, bt_merge=1, first_pos_is_zero=False, z=(zl_s, zr_s))
    yp, ys = _ffn(u2_p, x1_p, u2_s, x1_s, w['ffn_wg'], w['ffn_wu'], w['ffn_wd'], w['norm_final_g'],
                  bm=512, bf=512)
    st = lambda a, n: a.reshape(1, n, S5_G, S5_P)
    return (yp.reshape(batch, seq, D_MODEL), ys.reshape(dec_batch, 1, D_MODEL),
            conv_p[None], h_p[None], st(sr_p, batch), st(si_p, batch),
            conv_s[None], h_s[None], st(sr_s, dec_batch), st(si_s, dec_batch))
```

```python
import functools
import math

import jax
import jax.numpy as jnp
import numpy as np
from jax import lax
from jax.experimental import pallas as pl
from jax.experimental.pallas import tpu as pltpu

F32 = jnp.float32
BF16 = jnp.bfloat16

D_MODEL = 2048
D_LRU = D_MODEL
H_LRU = 8
LRU_BW = D_LRU // H_LRU
LRU_C = 8.0
CONV_W = 4
D_S5 = D_MODEL // 2
S5_H = 16
S5_G = D_S5 // S5_H
S5_P = 64
S5_N = S5_G * S5_P
D_FF = 5632
IN_WIDTH = D_LRU + D_S5 + 2 * D_MODEL
EPS = 1e-6

S5_SLAB_G = 8
S5_SLABS = S5_G // S5_SLAB_G
S5_SLAB_IN = S5_SLAB_G * S5_H
S5_SLAB_ST = S5_SLAB_G * S5_P

LANES = 128
SUBLANES = 8
HALF = SUBLANES // 2
VMEM_LIMIT_BYTES = 56 * 1024 * 1024


def _params(sem):
    return pltpu.CompilerParams(dimension_semantics=sem, vmem_limit_bytes=VMEM_LIMIT_BYTES)


def _const_spec(shape):
    nd = len(shape)
    return pl.BlockSpec(shape, lambda *_: (0,) * nd, pipeline_mode=pl.Buffered(1))


def _side_cast_specs(arrs, n_steps, step_of):
    specs, shapes = [], []
    for a in arrs:
        rows = a.shape[0] // n_steps
        assert rows * n_steps == a.shape[0] and rows % (2 * SUBLANES) == 0
        specs.append(pl.BlockSpec((rows, a.shape[1]), lambda *g: (step_of(*g), 0)))
        shapes.append(jax.ShapeDtypeStruct(a.shape, BF16))
    return specs, shapes


def _split_refs(refs, n_in, n_out, n_cast):
    cuts = [n_in, n_in + n_cast, n_in + n_cast + n_out, n_in + 2 * n_cast + n_out]
    parts = [refs[a:b] for a, b in zip([0] + cuts, cuts + [len(refs)])]
    return parts


def _rmsnorm(x, g):
    ms = jnp.mean(x * x, axis=-1, keepdims=True)
    return x * lax.rsqrt(ms + EPS) * g


LOG2E = 1.4426950408889634


def _sigmoid(x):
    return 1.0 / (1.0 + jnp.exp2(x * (-LOG2E)))


def _lo_rows(shape):
    return lax.broadcasted_iota(jnp.int32, shape, 0) % SUBLANES < HALF


def _s5prep_kernel(lre_ref, lim_ref, ldt_ref, br_ref, bi_ref,
                   bbr_ref, bbi_ref, abbr_ref, abbi_ref, abr_ref, abi_ref, a2r_ref, a2i_ref):
    lre = lre_ref[...]
    lim = lim_ref[...]
    dt = jnp.exp(ldt_ref[...])
    mag = jnp.exp(lre * dt)
    ab_re = mag * jnp.cos(lim * dt)
    ab_im = mag * jnp.sin(lim * dt)
    e_re = ab_re - 1.0
    e_im = ab_im
    den = lre * lre + lim * lim
    co_re = (e_re * lre + e_im * lim) / den
    co_im = (e_im * lre - e_re * lim) / den
    br = br_ref[...]
    bi = bi_ref[...]
    bb_re = co_re * br - co_im * bi
    bb_im = co_re * bi + co_im * br
    bbr_ref[...] = bb_re
    bbi_ref[...] = bb_im
    abbr_ref[...] = ab_re * bb_re - ab_im * bb_im
    abbi_ref[...] = ab_re * bb_im + ab_im * bb_re
    abr_ref[...] = ab_re
    abi_ref[...] = ab_im
    a2r_ref[...] = ab_re * ab_re - ab_im * ab_im
    a2i_ref[...] = 2.0 * (ab_re * ab_im)


def _s5_prepare(lam_re, lam_im, log_dt, b_re, b_im, c_re, c_im):
    full = (S5_G, S5_P, S5_H)
    flat = (S5_G * S5_P * S5_H // LANES, LANES)
    bc = lambda a: jnp.broadcast_to(a, full).reshape(flat)
    args = (bc(lam_re[:, :, None]), bc(lam_im[:, :, None]), bc(log_dt[:, None, None]),
            b_re.reshape(flat), b_im.reshape(flat))
    sds = jax.ShapeDtypeStruct(flat, F32)
    bbr, bbi, abbr, abbi, abr, abi, a2r, a2i = pl.pallas_call(
        _s5prep_kernel, out_shape=(sds,) * 8, name="s5prep")(*args)
    per_state = lambda a: a.reshape(full)[:, :, 0].reshape(1, S5_N)
    coeff = lambda a1, a2: jnp.concatenate(
        [jnp.broadcast_to(per_state(a1), (HALF, S5_N)), jnp.broadcast_to(per_state(a2), (HALF, S5_N))], 0)
    ca_re, ca_im = coeff(abr, a2r), coeff(abi, a2i)
    eye = np.eye(S5_SLAB_G, dtype=np.float32)

    def slab_b(re, im):
        b = jnp.stack([re.reshape(full), im.reshape(full)], 0)
        b = b.reshape(2, S5_SLABS, S5_SLAB_G, S5_P, S5_H)
        return jnp.einsum('cjgph,gk->jghckp', b, eye).reshape(S5_SLABS, S5_SLAB_IN, 2 * S5_SLAB_ST)

    wb = jnp.concatenate([slab_b(bbr, bbi), slab_b(abbr, abbi)], axis=1)
    cc = jnp.stack([c_re, -c_im], 0).reshape(2, S5_SLABS, S5_SLAB_G, S5_H, S5_P)
    wc = jnp.einsum('cjghp,gk->jcgpkh', cc, eye).reshape(S5_SLABS, 2 * S5_SLAB_ST, S5_SLAB_IN)
    return ca_re, ca_im, wb.astype(BF16), wc.astype(BF16)


PERM_T = 64


def _perm_matrix(nb, to_time_major):
    n = nb * PERM_T
    tm = np.arange(n)
    sm = (tm % nb) * PERM_T + tm // nb
    p = np.zeros((n, n), np.float32)
    p[tm, sm] = 1.0
    return jnp.asarray(p if to_time_major else p.T, dtype=BF16)


def _row_tile_spec(nb, bt, width):
    if bt == 1:
        return pl.BlockSpec((nb, width), lambda *g: (0, 0))
    return pl.BlockSpec((nb, bt, width), lambda *g: (0, g[0], 0))


N_LRU_BLOCKS = 2
INPROJ_BN = 1024
ZR_WIDTH = IN_WIDTH - D_LRU


def _zr_block(j):
    return jnp.where(j <= N_LRU_BLOCKS, (ZR_WIDTH - D_S5) // INPROJ_BN, j - N_LRU_BLOCKS - 1)


def _inproj_kernel(*refs, nb, bt, n_cast, emit_w):
    (x_ref, g_ref, perm_ref, w_ref), cast_src, outs, cast_dst, (u_scr,) = _split_refs(
        refs, 4, 2 + int(emit_w), n_cast)
    zl_ref, zr_ref = outs[:2]
    j = pl.program_id(1)

    @pl.when(j == 0)
    def _():
        for src, dst in zip(cast_src, cast_dst):
            dst[...] = src[...].astype(BF16)
        if bt == 1:
            u_scr[...] = _rmsnorm(x_ref[...], g_ref[...]).astype(BF16)
        else:
            for s in range(bt // PERM_T):
                ts = slice(s * PERM_T, (s + 1) * PERM_T)
                xin = jnp.concatenate([x_ref[b, ts, :] for b in range(nb)], axis=0)
                u = _rmsnorm(xin, g_ref[...]).astype(BF16)
                u_scr[s * nb * PERM_T:(s + 1) * nb * PERM_T, :] = jnp.dot(
                    perm_ref[...], u, preferred_element_type=F32).astype(BF16)

    w = w_ref[...]
    if emit_w:
        w = w.astype(BF16)
        outs[2][...] = w
    acc = jnp.dot(u_scr[...], w, preferred_element_type=F32)

    @pl.when(j < N_LRU_BLOCKS)
    def _():
        zl_ref[...] = acc

    @pl.when(j >= N_LRU_BLOCKS)
    def _():
        zr_ref[...] = acc.astype(BF16)


def _inproj(x, g, w, *, nb, seq_len, bt, cast=()):
    assert bt == 1 or bt % PERM_T == 0
    m, bm, bn = nb * seq_len, nb * bt, INPROJ_BN
    emit_w = w.dtype == F32
    assert not emit_w or m == bm
    w_spec = pl.BlockSpec((D_MODEL, bn), lambda i, j: (0, j))
    perm = _perm_matrix(nb, True) if bt > 1 else jnp.zeros((SUBLANES, LANES), BF16)
    cast_specs, cast_shapes = _side_cast_specs(cast, m // bm, lambda i, j: i)
    return pl.pallas_call(
        functools.partial(_inproj_kernel, nb=nb, bt=bt, n_cast=len(cast), emit_w=emit_w),
        grid=(m // bm, IN_WIDTH // bn),
        in_specs=[_row_tile_spec(nb, bt, D_MODEL),
                  pl.BlockSpec((1, D_MODEL), lambda i, j: (0, 0)),
                  _const_spec(perm.shape),
                  w_spec] + cast_specs,
        out_specs=[pl.BlockSpec((bm, bn), lambda i, j: (i, jnp.minimum(j, N_LRU_BLOCKS - 1))),
                   pl.BlockSpec((bm, bn), lambda i, j: (i, _zr_block(j)))]
        + [w_spec] * int(emit_w) + cast_specs,
        out_shape=[jax.ShapeDtypeStruct((m, D_LRU), F32), jax.ShapeDtypeStruct((m, ZR_WIDTH), BF16)]
        + [jax.ShapeDtypeStruct(w.shape, BF16)] * int(emit_w) + cast_shapes,
        scratch_shapes=[pltpu.VMEM((bm, D_MODEL), BF16)],
        compiler_params=_params(("parallel", "arbitrary")),
        name="inproj",
    )(x, g, perm, w, *cast)


def _lru_kernel(*refs, nb, bt, n_tiles, first_pos_is_zero, n_cast):
    ((x_ref, conv0_ref, h0_ref, cw_ref, cb_ref, wax_ref, bax_ref, lam_ref, proj_ref),
     cast_src, (ya_ref, conv_ref, ht_ref), cast_dst,
     (xe_scr, a_scr, b_scr, hc_scr, hq_scr)) = _split_refs(refs, 9, 3, n_cast)
    c = pl.program_id(0)
    rows = nb * bt
    pad = 2 * SUBLANES
    pipelined = bt > 1

    @pl.when(c == 0)
    def _():
        conv_ref[...] = conv0_ref[...]
        ht_ref[...] = h0_ref[...]
        if pipelined:
            xe_scr[:pad - 3 * nb, :] = jnp.zeros((pad - 3 * nb, D_LRU), F32)
            hq_scr[...] = jnp.zeros(hq_scr.shape, BF16)

    nlam = -lam_ref[...]
    sp = jnp.maximum(nlam, 0.0) + jnp.log1p(jnp.exp(-jnp.abs(nlam)))
    k_tanh = LRU_C * sp
    k_exp2 = (-LRU_C * LOG2E) * sp

    if bt > 1:
        for k in range(CONV_W - 1):
            xe_scr[pad - (3 - k) * nb:pad - (2 - k) * nb, :] = conv_ref[:, k, :]
        xe_scr[pad:, :] = x_ref[...]
        is_first = (lax.broadcasted_iota(jnp.int32, (rows, LRU_BW), 0) < nb) & (c == 0)

    for hb in range(H_LRU):
        cs = slice(hb * LRU_BW, (hb + 1) * LRU_BW)
        if pipelined:
            ya = jnp.dot(hq_scr[...], proj_ref[:, cs], preferred_element_type=F32)
            ya_ref[:, cs] = ya.astype(BF16)
        if hb < n_cast:
            cast_dst[hb][...] = cast_src[hb][...].astype(BF16)
        if bt > 1:
            xs = pltpu.roll(xe_scr[:, cs], nb, axis=0)
            xc = (cb_ref[:, cs] + xs[pad - 2 * nb:pad + rows - 2 * nb] * cw_ref[0:1, cs]
                  + xe_scr[pad - 2 * nb:pad + rows - 2 * nb, cs] * cw_ref[1:2, cs]
                  + xs[pad:] * cw_ref[2:3, cs]
                  + xe_scr[pad:, cs] * cw_ref[3:4, cs])
        else:
            xc = (cb_ref[:, cs] + conv_ref[:, 0, cs] * cw_ref[0:1, cs]
                  + conv_ref[:, 1, cs] * cw_ref[1:2, cs] + conv_ref[:, 2, cs] * cw_ref[2:3, cs]
                  + x_ref[:, cs] * cw_ref[3:4, cs])
        gates = jnp.dot(xc.astype(BF16), wax_ref[hb], preferred_element_type=F32) + bax_ref[hb]
        r = _sigmoid(gates[:, :LRU_BW])
        i = _sigmoid(gates[:, LRU_BW:])
        a = jnp.exp2(r * k_exp2[:, cs])
        m2 = jnp.tanh(r * k_tanh[:, cs]) * (1.0 + a * a)
        mult = jnp.where(m2 > 0.0, m2 * lax.rsqrt(m2), 0.0)
        if first_pos_is_zero:
            mult = jnp.where(is_first, 1.0, mult)
        a_scr[:, cs] = a
        b_scr[:, cs] = mult * (i * xc)

    if not pipelined:
        conv_ref[:, 0, :] = conv_ref[:, 1, :]
        conv_ref[:, 1, :] = conv_ref[:, 2, :]
        conv_ref[:, 2, :] = x_ref[...]
        h = a_scr[...] * ht_ref[...] + b_scr[...]
        ht_ref[...] = h
        ya = jnp.dot(h.astype(BF16), proj_ref[...], preferred_element_type=F32)
        ya_ref[...] = ya.astype(BF16)
        return

    @pl.when(c < n_tiles)
    def _():
        for k in range(CONV_W - 1):
            conv_ref[:, k, :] = x_ref[rows - (3 - k) * nb:rows - (2 - k) * nb, :]

        n_cb = D_LRU // LANES
        lo = _lo_rows((SUBLANES, LANES))
        hc_scr[:HALF, :] = ht_ref[...]
        hc_scr[HALF:, :] = ht_ref[...]

        def pair(k, hs):
            rs = pl.ds(pl.multiple_of(k * SUBLANES, SUBLANES), SUBLANES)
            out = []
            for cb in range(n_cb):
                cl = slice(cb * LANES, (cb + 1) * LANES)
                av, bv = a_scr[rs, cl], b_scr[rs, cl]
                first = av * hs[cb] + bv
                second = av * pltpu.roll(first, HALF, axis=0) + bv
                h = jnp.where(lo, first, second)
                b_scr[rs, cl] = h
                out.append(pltpu.roll(h, HALF, axis=0))
            return tuple(out)

        hs = tuple(hc_scr[:, cb * LANES:(cb + 1) * LANES] for cb in range(n_cb))
        hs = lax.fori_loop(0, rows // SUBLANES, pair, hs)
        for cb in range(n_cb):
            hc_scr[:, cb * LANES:(cb + 1) * LANES] = hs[cb]
        ht_ref[...] = hc_scr[:HALF, :]
        hq_scr[...] = b_scr[...].astype(BF16)


def _lru(z, conv0, h0, cw, cb, wax, bax, lam, proj, *, nb, seq_len, bt, first_pos_is_zero, cast=()):
    assert bt == 1 or (nb == HALF and bt % 2 == 0 and bt >= 4)
    assert len(cast) <= H_LRU
    rows, n_tiles = nb * bt, seq_len // bt
    pipelined = bt > 1
    st = pl.BlockSpec((nb, D_LRU), lambda c: (0, 0))
    st_sds = jax.ShapeDtypeStruct((nb, D_LRU), F32)
    cst = pl.BlockSpec((nb, CONV_W - 1, D_LRU), lambda c: (0, 0, 0))
    cst_sds = jax.ShapeDtypeStruct((nb, CONV_W - 1, D_LRU), F32)
    x_tile = pl.BlockSpec((rows, D_LRU), lambda c: (jnp.minimum(c, n_tiles - 1), 0))
    ya_tile = pl.BlockSpec((rows, D_LRU), lambda c: (jnp.maximum(c - int(pipelined), 0), 0))
    kern = functools.partial(_lru_kernel, nb=nb, bt=bt, n_tiles=n_tiles,
                             first_pos_is_zero=first_pos_is_zero, n_cast=len(cast))
    cast_specs, cast_shapes = _side_cast_specs(cast, n_tiles, lambda c: jnp.minimum(c, n_tiles - 1))
    small = (SUBLANES, LANES)
    ya, conv, ht, *cast_out = pl.pallas_call(
        kern,
        grid=(n_tiles + int(pipelined),),
        in_specs=[x_tile, cst, st,
                  _const_spec((CONV_W, D_LRU)), _const_spec((1, D_LRU)),
                  _const_spec((H_LRU, LRU_BW, 2 * LRU_BW)), _const_spec((H_LRU, 1, 2 * LRU_BW)),
                  _const_spec((1, D_LRU)), _const_spec((D_LRU, D_MODEL))] + cast_specs,
        out_specs=[ya_tile, cst, st] + cast_specs,
        out_shape=[jax.ShapeDtypeStruct((nb * seq_len, D_LRU), BF16), cst_sds, st_sds] + cast_shapes,
        scratch_shapes=[pltpu.VMEM((rows + 2 * SUBLANES, D_LRU) if pipelined else small, F32),
                        pltpu.VMEM((rows, D_LRU), F32), pltpu.VMEM((rows, D_LRU), F32),
                        pltpu.VMEM((SUBLANES, D_LRU), F32),
                        pltpu.VMEM((rows, D_LRU) if pipelined else (2 * SUBLANES, LANES), BF16)],
        compiler_params=_params(("arbitrary",)),
        name="lru",
    )(z, conv0, h0, cw, cb, wax, bax, lam, proj, *cast)
    return ya, conv, ht, cast_out


def _gelu_tanh(x):
    c = math.sqrt(2.0 / math.pi)
    return 0.5 * x * (1.0 + jnp.tanh(c * (x + 0.044715 * (x * x * x))))


S5_GROUPS = 4
S5_GROUP_SLABS = S5_SLABS // S5_GROUPS
S5_GROUP_COLS = S5_N // S5_GROUPS
GLU_GROUP_COLS = D_MODEL // S5_GROUPS


def _s5_b_proj(ub, up, wb_ref, xr_scr, xi_scr, j):
    js = slice(j * S5_SLAB_IN, (j + 1) * S5_SLAB_IN)
    if up is None:
        w = jnp.dot(ub[:, js], wb_ref[j, :S5_SLAB_IN, :], preferred_element_type=F32)
    else:
        w = jnp.dot(jnp.concatenate([ub[:, js], up[:, js]], axis=1), wb_ref[j],
                    preferred_element_type=F32)
    xr_scr[:, j * S5_SLAB_ST:(j + 1) * S5_SLAB_ST] = w[:, :S5_SLAB_ST]
    xi_scr[:, j * S5_SLAB_ST:(j + 1) * S5_SLAB_ST] = w[:, S5_SLAB_ST:]


def _s5_c_proj(xr_scr, xi_scr, wc_ref, y_scr, j):
    ss = slice(j * S5_SLAB_ST, (j + 1) * S5_SLAB_ST)
    xs = jnp.concatenate([xr_scr[:, ss].astype(BF16), xi_scr[:, ss].astype(BF16)], axis=1)
    y_scr[:, j * S5_SLAB_IN:(j + 1) * S5_SLAB_IN] = jnp.dot(xs, wc_ref[j], preferred_element_type=F32)


def _s5_kernel(*refs, nb, bt, n_tiles, n_cast):
    ((u_ref, s0r_ref, s0i_ref, car_ref, cai_ref, wb_ref, wc_ref, d_ref, wv_ref, wg_ref), cast_src,
     (yb_ref, sr_ref, si_ref), cast_dst,
     (xr_scr, xi_scr, pr_scr, pi_scr, y_scr, v_scr, ulast_scr)) = _split_refs(refs, 10, 3, n_cast)
    c = pl.program_id(0)
    rows = nb * bt
    ub = u_ref[...]
    u = ub.astype(F32)

    if bt == 1:
        for j in range(S5_SLABS):
            _s5_b_proj(ub, None, wb_ref, xr_scr, xi_scr, j)
        ar, ai = car_ref[0:1, :], cai_ref[0:1, :]
        pr, pi = s0r_ref[...], s0i_ref[...]
        nr = ar * pr - ai * pi + xr_scr[...]
        ni = ar * pi + ai * pr + xi_scr[...]
        xr_scr[...] = nr
        xi_scr[...] = ni
        sr_ref[...] = nr
        si_ref[...] = ni
        for j in range(S5_SLABS):
            _s5_c_proj(xr_scr, xi_scr, wc_ref, y_scr, j)
        v = _gelu_tanh(y_scr[...] + d_ref[...] * u).astype(BF16)
        yv = jnp.dot(v, wv_ref[...], preferred_element_type=F32)
        yg = jnp.dot(v, wg_ref[...], preferred_element_type=F32)
        yb_ref[...] = (yv * _sigmoid(yg)).astype(BF16)
        return

    @pl.when(c == 0)
    def _():
        pr_scr[...] = jnp.zeros(pr_scr.shape, F32)
        pi_scr[...] = jnp.zeros(pi_scr.shape, F32)
        ulast_scr[...] = jnp.zeros(ulast_scr.shape, F32)
        v_scr[...] = jnp.zeros(v_scr.shape, BF16)

    up = pltpu.roll(jnp.concatenate([ulast_scr[...], u], axis=0), HALF, axis=0)[SUBLANES:]
    up = up.astype(BF16)
    ulast_scr[...] = u[rows - SUBLANES:, :]
    lo = _lo_rows((SUBLANES, LANES))
    first = c == 0
    v_prev = v_scr[...]
    for grp in range(S5_GROUPS):
        gs = slice(grp * GLU_GROUP_COLS, (grp + 1) * GLU_GROUP_COLS)
        yv = jnp.dot(v_prev, wv_ref[:, gs], preferred_element_type=F32)
        yg = jnp.dot(v_prev, wg_ref[:, gs], preferred_element_type=F32)
        yb_ref[:, gs] = (yv * _sigmoid(yg)).astype(BF16)
        if grp < n_cast:
            cast_dst[grp][...] = cast_src[grp][...].astype(BF16)

        slabs = range(grp * S5_GROUP_SLABS, (grp + 1) * S5_GROUP_SLABS)
        for j in slabs:
            _s5_b_proj(ub, up, wb_ref, xr_scr, xi_scr, j)
        cols = [slice(grp * S5_GROUP_COLS + cb * LANES, grp * S5_GROUP_COLS + (cb + 1) * LANES)
                for cb in range(S5_GROUP_COLS // LANES)]
        carry, coeff = [], []
        for cl in cols:
            ar, ai = car_ref[:, cl], cai_ref[:, cl]
            s0r = jnp.concatenate([s0r_ref[:, cl], s0r_ref[:, cl]], axis=0)
            s0i = jnp.concatenate([s0i_ref[:, cl], s0i_ref[:, cl]], axis=0)
            xr_scr[:SUBLANES, cl] += jnp.where(first, ar * s0r - ai * s0i, 0.0)
            xi_scr[:SUBLANES, cl] += jnp.where(first, ar * s0i + ai * s0r, 0.0)
            coeff.append((jnp.where(lo, pltpu.roll(ar, HALF, axis=0), ar),
                          jnp.where(lo, pltpu.roll(ai, HALF, axis=0), ai)))
            carry.append((pr_scr[:, cl], pi_scr[:, cl]))
        for k in range(rows // SUBLANES):
            rs = slice(k * SUBLANES, (k + 1) * SUBLANES)
            for cb, cl in enumerate(cols):
                (pr, pi), (ar, ai) = carry[cb], coeff[cb]
                nr = ar * pr - ai * pi + xr_scr[rs, cl]
                ni = ar * pi + ai * pr + xi_scr[rs, cl]
                xr_scr[rs, cl] = nr
                xi_scr[rs, cl] = ni
                carry[cb] = (nr, ni)
        for cb, cl in enumerate(cols):
            pr_scr[:, cl], pi_scr[:, cl] = carry[cb]
        for j in slabs:
            _s5_c_proj(xr_scr, xi_scr, wc_ref, y_scr, j)
    v_scr[...] = _gelu_tanh(y_scr[...] + d_ref[...] * u).astype(BF16)

    @pl.when(c == n_tiles - 1)
    def _():
        sr_ref[...] = pr_scr[HALF:, :]
        si_ref[...] = pi_scr[HALF:, :]


def _s5(z, s0r, s0i, ca_re, ca_im, wb, wc, d, wv, wg, *, nb, seq_len, bt, cast=()):
    assert bt == 1 or (nb == HALF and bt % 2 == 0)
    assert len(cast) <= S5_GROUPS
    rows, n_tiles = nb * bt, seq_len // bt
    pipelined = bt > 1
    st = pl.BlockSpec((nb, S5_N), lambda c: (0, 0))
    st_sds = jax.ShapeDtypeStruct((nb, S5_N), F32)
    x_s5_block = (ZR_WIDTH - D_S5) // D_S5
    cast_specs, cast_shapes = _side_cast_specs(cast, n_tiles, lambda c: jnp.minimum(c, n_tiles - 1))
    yb, sr, si, *cast_out = pl.pallas_call(
        functools.partial(_s5_kernel, nb=nb, bt=bt, n_tiles=n_tiles, n_cast=len(cast)),
        grid=(n_tiles + int(pipelined),),
        in_specs=[pl.BlockSpec((rows, D_S5), lambda c: (jnp.minimum(c, n_tiles - 1), x_s5_block)),
                  st, st, _const_spec((SUBLANES, S5_N)), _const_spec((SUBLANES, S5_N)),
                  _const_spec((S5_SLABS, 2 * S5_SLAB_IN, 2 * S5_SLAB_ST)),
                  _const_spec((S5_SLABS, 2 * S5_SLAB_ST, S5_SLAB_IN)),
                  _const_spec((1, D_S5)), _const_spec((D_S5, D_MODEL)),
                  _const_spec((D_S5, D_MODEL))] + cast_specs,
        out_specs=[pl.BlockSpec((rows, D_MODEL), lambda c: (jnp.maximum(c - int(pipelined), 0), 0)),
                   st, st] + cast_specs,
        out_shape=[jax.ShapeDtypeStruct((nb * seq_len, D_MODEL), BF16), st_sds, st_sds] + cast_shapes,
        scratch_shapes=[pltpu.VMEM((rows, S5_N), F32), pltpu.VMEM((rows, S5_N), F32),
                        pltpu.VMEM((SUBLANES, S5_N), F32), pltpu.VMEM((SUBLANES, S5_N), F32),
                        pltpu.VMEM((rows, D_S5), F32), pltpu.VMEM((rows, D_S5), BF16),
                        pltpu.VMEM((SUBLANES, D_S5), F32)],
        compiler_params=_params(("arbitrary",)),
        name="s5",
    )(z, s0r, s0i, ca_re, ca_im, wb, wc, d, wv, wg, *cast)
    return yb, sr, si, cast_out


def _merge_kernel(ya_ref, yb_ref, ga_ref, gb_ref, x_ref, perm_ref, w_ref, g_ref, x1_ref, u2_ref,
                  *, nb, bt):
    def gated(rs):
        f32 = lambda ref: ref[rs, :].astype(F32)
        merged = _sigmoid(f32(ga_ref)) * f32(ya_ref) + _sigmoid(f32(gb_ref)) * f32(yb_ref)
        return merged.astype(BF16)

    if bt == 1:
        x1 = x_ref[...] + jnp.dot(gated(slice(None)), w_ref[...], preferred_element_type=F32)
        x1_ref[...] = x1
        u2_ref[...] = _rmsnorm(x1, g_ref[...]).astype(BF16)
        return
    n_sub = bt // PERM_T
    sub_rows = lambda s: slice(s * nb * PERM_T, (s + 1) * nb * PERM_T)

    def finish(s, delta):
        ts = slice(s * PERM_T, (s + 1) * PERM_T)
        for b in range(nb):
            x1 = x_ref[b, ts, :] + delta[b * PERM_T:(b + 1) * PERM_T]
            x1_ref[b, ts, :] = x1
            u2_ref[b, ts, :] = _rmsnorm(x1, g_ref[...]).astype(BF16)

    merged = gated(sub_rows(0))
    pending = None
    for s in range(n_sub):
        mb = jnp.dot(perm_ref[...], merged, preferred_element_type=F32).astype(BF16)
        delta = jnp.dot(mb, w_ref[...], preferred_element_type=F32)
        if s + 1 < n_sub:
            merged = gated(sub_rows(s + 1))
        if pending is not None:
            finish(*pending)
        pending = (s, delta)
    finish(*pending)


def _merge(ya, yb, zr, x, w_out, g, *, nb, seq_len, bt):
    assert bt == 1 or bt % PERM_T == 0
    bm = nb * bt
    row = lambda col: pl.BlockSpec((bm, D_MODEL), lambda i: (i, col))
    perm = _perm_matrix(nb, False) if bt > 1 else jnp.zeros((SUBLANES, LANES), BF16)
    tile = _row_tile_spec(nb, bt, D_MODEL)
    return pl.pallas_call(
        functools.partial(_merge_kernel, nb=nb, bt=bt),
        grid=(seq_len // bt,),
        in_specs=[row(0), row(0), row(0), row(1), tile, _const_spec(perm.shape),
                  _const_spec((D_MODEL, D_MODEL)), _const_spec((1, D_MODEL))],
        out_specs=(tile, tile),
        out_shape=(jax.ShapeDtypeStruct(x.shape, F32), jax.ShapeDtypeStruct(x.shape, BF16)),
        compiler_params=_params(("parallel",)),
        name="merge",
    )(ya, yb, zr, zr, x, perm, w_out, g)


def _ffn_kernel(u_ref, x1_ref, ue_ref, x1e_ref, wg_ref, wu_ref, wd_ref, g_ref, y_ref, ye_ref,
                acc_scr, acce_scr):
    i, f = pl.program_id(0), pl.program_id(1)
    first, last = f == 0, f == pl.num_programs(1) - 1
    with_extra = i == 0
    bm = u_ref.shape[0]

    def down_proj(u):
        hg = jnp.dot(u, wg_ref[...], preferred_element_type=F32)
        hu = jnp.dot(u, wu_ref[...], preferred_element_type=F32)
        hid = (hg * _sigmoid(hg) * hu).astype(BF16)
        return jnp.dot(hid, wd_ref[...], preferred_element_type=F32)

    @pl.when(first)
    def _():
        acc_scr[...] = x1_ref[...]

    @pl.when(first & with_extra)
    def _():
        acce_scr[...] = x1e_ref[...]

    @pl.when(with_extra)
    def _():
        down = down_proj(jnp.concatenate([u_ref[...], ue_ref[...]], axis=0))
        acc_scr[...] += down[:bm]
        acce_scr[...] += down[bm:]

    @pl.when(jnp.logical_not(with_extra))
    def _():
        acc_scr[...] += down_proj(u_ref[...])

    @pl.when(last)
    def _():
        y_ref[...] = _rmsnorm(acc_scr[...], g_ref[...])

    @pl.when(last & with_extra)
    def _():
        ye_ref[...] = _rmsnorm(acce_scr[...], g_ref[...])


def _ffn(u2, x1, u2_extra, x1_extra, wg, wu, wd, g, bm, bf):
    m, me = x1.shape[0], x1_extra.shape[0]
    row = pl.BlockSpec((bm, D_MODEL), lambda i, f: (i, 0))
    extra = pl.BlockSpec((me, D_MODEL), lambda i, f: (0, 0))
    return pl.pallas_call(
        _ffn_kernel,
        grid=(m // bm, D_FF // bf),
        in_specs=[row, row, extra, extra,
                  pl.BlockSpec((D_MODEL, bf), lambda i, f: (0, f)),
                  pl.BlockSpec((D_MODEL, bf), lambda i, f: (0, f)),
                  pl.BlockSpec((bf, D_MODEL), lambda i, f: (f, 0)),
                  pl.BlockSpec((1, D_MODEL), lambda i, f: (0, 0))],
        out_specs=(row, extra),
        out_shape=(jax.ShapeDtypeStruct((m, D_MODEL), F32), jax.ShapeDtypeStruct((me, D_MODEL), F32)),
        scratch_shapes=[pltpu.VMEM((bm, D_MODEL), F32), pltpu.VMEM((me, D_MODEL), F32)],
        compiler_params=_params(("arbitrary", "arbitrary")),
        name="ffn",
    )(u2, x1, u2_extra, x1_extra, wg, wu, wd, g)


def _mixer(x, conv0, h0, s0r, s0i, w, w32, *, bt_in, bt_lru, bt_s5, bt_merge, first_pos_is_zero,
           z=None):
    nb, seq_len = (x.shape[0], 1) if x.ndim == 2 else x.shape[:2]
    seq = dict(nb=nb, seq_len=seq_len)
    w = dict(w)
    take = lambda *names: tuple(w32[n] for n in names) if w32 else ()
    keep = lambda names, arrs: w.update(zip(names, arrs)) if w32 else None

    if z is None:
        zl, zr, *cast = _inproj(x, w['norm_mix_g'], w['w_in'], bt=bt_in, cast=take('proj'), **seq)
        keep(('proj',), cast)
    else:
        zl, zr = z
    ya, conv_n, h_n, cast = _lru(zl, conv0, h0, w['cw'], w['cb'], w['wax'], w['bax'], w['lam'],
                                 w['proj'], bt=bt_lru, first_pos_is_zero=first_pos_is_zero,
                                 cast=take('w_out', 'wv', 'wg'), **seq)
    keep(('w_out', 'wv', 'wg'), cast)
    yb, sr, si, cast = _s5(zr, s0r, s0i, w['ca_re'], w['ca_im'], w['wb'], w['wc'], w['d'], w['wv'],
                           w['wg'], bt=bt_s5, cast=take('ffn_wg', 'ffn_wu', 'ffn_wd'), **seq)
    keep(('ffn_wg', 'ffn_wu', 'ffn_wd'), cast)
    x1, u2 = _merge(ya, yb, zr, x, w['w_out'], w['norm_ffn_g'], bt=bt_merge, **seq)
    m = nb * seq_len
    return (x1.reshape(m, D_MODEL), u2.reshape(m, D_MODEL), conv_n, h_n, sr, si), w


def kernel(x_prompt, x_sample, state_lru_conv, state_lru_h, state_s5_re, state_s5_im, norm_mix_g, w_in, lru_conv_w, lru_conv_b, lru_wa, lru_ba, lru_wx, lru_bx, lru_lambda, lru_proj, s5_lambda_re, s5_lambda_im, s5_log_dt, s5_b_re, s5_b_im, s5_c_re, s5_c_im, s5_d, s5_glu_wv, s5_glu_wg, w_out, norm_ffn_g, ffn_w_gate, ffn_w_up, ffn_w_down, norm_final_g):
    batch, seq, _ = x_prompt.shape
    dec_batch = x_sample.shape[0]
    l = 0
    ca_re, ca_im, wb, wc = _s5_prepare(s5_lambda_re[l], s5_lambda_im[l], s5_log_dt[l],
                                       s5_b_re[l], s5_b_im[l], s5_c_re[l], s5_c_im[l])
    w = dict(
        norm_mix_g=norm_mix_g[l][None, :],
        cw=lru_conv_w[l], cb=lru_conv_b[l][None, :],
        wax=jnp.concatenate([lru_wa[l], lru_wx[l]], axis=-1).astype(BF16),
        bax=jnp.concatenate([lru_ba[l], lru_bx[l]], axis=-1)[:, None, :],
        lam=lru_lambda[l][None, :],
        ca_re=ca_re, ca_im=ca_im, wb=wb, wc=wc, d=s5_d[l].reshape(1, D_S5),
        norm_ffn_g=norm_ffn_g[l][None, :], norm_final_g=norm_final_g[None, :],
    )
    w32 = dict(proj=lru_proj[l], w_out=w_out[l], wv=s5_glu_wv[l], wg=s5_glu_wg[l],
               ffn_wg=ffn_w_gate[l], ffn_wu=ffn_w_up[l], ffn_wd=ffn_w_down[l])
    zeros = lambda *s: jnp.zeros(s, F32)
    xs = x_sample.reshape(dec_batch, D_MODEL)
    zl_s, zr_s, w['w_in'] = _inproj(xs, w['norm_mix_g'], w_in[l], nb=dec_batch, seq_len=1, bt=1)
    (x1_p, u2_p, conv_p, h_p, sr_p, si_p), w = _mixer(
        x_prompt, zeros(batch, CONV_W - 1, D_LRU), zeros(batch, D_LRU),
        zeros(batch, S5_N), zeros(batch, S5_N), w, w32,
        bt_in=256, bt_lru=128, bt_s5=64, bt_merge=128, first_pos_is_zero=True)
    (x1_s, u2_s, conv_s, h_s, sr_s, si_s), _ = _mixer(
        xs, state_lru_conv[l], state_lru_h[l],
        state_s5_re[l].reshape(dec_batch, S5_N), state_s5_im[l].reshape(dec_batch, S5_N), w, None,
        bt_in=1, bt_lru=1, bt_s5=1, bt_merge=1, first_pos_is_zero=False, z=(zl_s, zr_s))
    yp, ys = _ffn(u2_p, x1_p, u2_s, x1_s, w['ffn_wg'], w['ffn_wu'], w['ffn_wd'], w['norm_final_g'],
                  bm=512, bf=512)
    st = lambda a, n: a.reshape(1, n, S5_G, S5_P)
    return (yp.reshape(batch, seq, D_MODEL), ys.reshape(dec_batch, 1, D_MODEL),
            conv_p[None], h_p[None], st(sr_p, batch), st(si_p, batch),
            conv_s[None], h_s[None], st(sr_s, dec_batch), st(si_s, dec_batch))
```

```python
import functools
import math

import jax
import jax.numpy as jnp
import numpy as np
from jax import lax
from jax.experimental import pallas as pl
from jax.experimental.pallas import tpu as pltpu

F32 = jnp.float32
BF16 = jnp.bfloat16

D_MODEL = 2048
D_LRU = D_MODEL
H_LRU = 8
LRU_BW = D_LRU // H_LRU
LRU_C = 8.0
CONV_W = 4
D_S5 = D_MODEL // 2
S5_H = 16
S5_G = D_S5 // S5_H
S5_P = 64
S5_N = S5_G * S5_P
D_FF = 5632
IN_WIDTH = D_LRU + D_S5 + 2 * D_MODEL
EPS = 1e-6

S5_SLAB_G = 8
S5_SLABS = S5_G // S5_SLAB_G
S5_SLAB_IN = S5_SLAB_G * S5_H
S5_SLAB_ST = S5_SLAB_G * S5_P

LANES = 128
SUBLANES = 8
HALF = SUBLANES // 2
VMEM_LIMIT_BYTES = 56 * 1024 * 1024


def _params(sem):
    return pltpu.CompilerParams(dimension_semantics=sem, vmem_limit_bytes=VMEM_LIMIT_BYTES)


def _const_spec(shape):
    nd = len(shape)
    return pl.BlockSpec(shape, lambda *_: (0,) * nd, pipeline_mode=pl.Buffered(1))


def _side_cast_specs(arrs, n_steps, step_of):
    specs, shapes = [], []
    for a in arrs:
        rows = a.shape[0] // n_steps
        assert rows * n_steps == a.shape[0] and rows % (2 * SUBLANES) == 0
        specs.append(pl.BlockSpec((rows, a.shape[1]), lambda *g: (step_of(*g), 0)))
        shapes.append(jax.ShapeDtypeStruct(a.shape, BF16))
    return specs, shapes


def _split_refs(refs, n_in, n_out, n_cast):
    cuts = [n_in, n_in + n_cast, n_in + n_cast + n_out, n_in + 2 * n_cast + n_out]
    parts = [refs[a:b] for a, b in zip([0] + cuts, cuts + [len(refs)])]
    return parts


def _rmsnorm(x, g):
    ms = jnp.mean(x * x, axis=-1, keepdims=True)
    return x * lax.rsqrt(ms + EPS) * g


LOG2E = 1.4426950408889634


def _sigmoid(x):
    return 1.0 / (1.0 + jnp.exp2(x * (-LOG2E)))


def _lo_rows(shape):
    return lax.broadcasted_iota(jnp.int32, shape, 0) % SUBLANES < HALF


def _s5prep_kernel(lre_ref, lim_ref, ldt_ref, br_ref, bi_ref,
                   bbr_ref, bbi_ref, abbr_ref, abbi_ref, abr_ref, abi_ref, a2r_ref, a2i_ref):
    lre = lre_ref[...]
    lim = lim_ref[...]
    dt = jnp.exp(ldt_ref[...])
    mag = jnp.exp(lre * dt)
    ab_re = mag * jnp.cos(lim * dt)
    ab_im = mag * jnp.sin(lim * dt)
    e_re = ab_re - 1.0
    e_im = ab_im
    den = lre * lre + lim * lim
    co_re = (e_re * lre + e_im * lim) / den
    co_im = (e_im * lre - e_re * lim) / den
    br = br_ref[...]
    bi = bi_ref[...]
    bb_re = co_re * br - co_im * bi
    bb_im = co_re * bi + co_im * br
    bbr_ref[...] = bb_re
    bbi_ref[...] = bb_im
    abbr_ref[...] = ab_re * bb_re - ab_im * bb_im
    abbi_ref[...] = ab_re * bb_im + ab_im * bb_re
    abr_ref[...] = ab_re
    abi_ref[...] = ab_im
    a2r_ref[...] = ab_re * ab_re - ab_im * ab_im
    a2i_ref[...] = 2.0 * (ab_re * ab_im)


def _s5_prepare(lam_re, lam_im, log_dt, b_re, b_im, c_re, c_im):
    full = (S5_G, S5_P, S5_H)
    flat = (S5_G * S5_P * S5_H // LANES, LANES)
    bc = lambda a: jnp.broadcast_to(a, full).reshape(flat)
    args = (bc(lam_re[:, :, None]), bc(lam_im[:, :, None]), bc(log_dt[:, None, None]),
            b_re.reshape(flat), b_im.reshape(flat))
    sds = jax.ShapeDtypeStruct(flat, F32)
    bbr, bbi, abbr, abbi, abr, abi, a2r, a2i = pl.pallas_call(
        _s5prep_kernel, out_shape=(sds,) * 8, name="s5prep")(*args)
    per_state = lambda a: a.reshape(full)[:, :, 0].reshape(1, S5_N)
    coeff = lambda a1, a2: jnp.concatenate(
        [jnp.broadcast_to(per_state(a1), (HALF, S5_N)), jnp.broadcast_to(per_state(a2), (HALF, S5_N))], 0)
    ca_re, ca_im = coeff(abr, a2r), coeff(abi, a2i)
    eye = np.eye(S5_SLAB_G, dtype=np.float32)

    def slab_b(re, im):
        b = jnp.stack([re.reshape(full), im.reshape(full)], 0)
        b = b.reshape(2, S5_SLABS, S5_SLAB_G, S5_P, S5_H)
        return jnp.einsum('cjgph,gk->jghckp', b, eye).reshape(S5_SLABS, S5_SLAB_IN, 2 * S5_SLAB_ST)

    wb = jnp.concatenate([slab_b(bbr, bbi), slab_b(abbr, abbi)], axis=1)
    cc = jnp.stack([c_re, -c_im], 0).reshape(2, S5_SLABS, S5_SLAB_G, S5_H, S5_P)
    wc = jnp.einsum('cjghp,gk->jcgpkh', cc, eye).reshape(S5_SLABS, 2 * S5_SLAB_ST, S5_SLAB_IN)
    return ca_re, ca_im, wb.astype(BF16), wc.astype(BF16)


PERM_T = 64


def _perm_matrix(nb, to_time_major):
    n = nb * PERM_T
    tm = np.arange(n)
    sm = (tm % nb) * PERM_T + tm // nb
    p = np.zeros((n, n), np.float32)
    p[tm, sm] = 1.0
    return jnp.asarray(p if to_time_major else p.T, dtype=BF16)


def _row_tile_spec(nb, bt, width):
    if bt == 1:
        return pl.BlockSpec((nb, width), lambda *g: (0, 0))
    return pl.BlockSpec((nb, bt, width), lambda *g: (0, g[0], 0))


N_LRU_BLOCKS = 2
INPROJ_BN = 1024
ZR_WIDTH = IN_WIDTH - D_LRU


def _zr_block(j):
    return jnp.where(j <= N_LRU_BLOCKS, (ZR_WIDTH - D_S5) // INPROJ_BN, j - N_LRU_BLOCKS - 1)


def _inproj_kernel(*refs, nb, bt, n_cast, emit_w):
    (x_ref, g_ref, perm_ref, w_ref), cast_src, outs, cast_dst, (u_scr,) = _split_refs(
        refs, 4, 2 + int(emit_w), n_cast)
    zl_ref, zr_ref = outs[:2]
    j = pl.program_id(1)

    @pl.when(j == 0)
    def _():
        for src, dst in zip(cast_src, cast_dst):
            dst[...] = src[...].astype(BF16)
        if bt == 1:
            u_scr[...] = _rmsnorm(x_ref[...], g_ref[...]).astype(BF16)
        else:
            for s in range(bt // PERM_T):
                ts = slice(s * PERM_T, (s + 1) * PERM_T)
                xin = jnp.concatenate([x_ref[b, ts, :] for b in range(nb)], axis=0)
                u = _rmsnorm(xin, g_ref[...]).astype(BF16)
                u_scr[s * nb * PERM_T:(s + 1) * nb * PERM_T, :] = jnp.dot(
                    perm_ref[...], u, preferred_element_type=F32).astype(BF16)

    w = w_ref[...]
    if emit_w:
        w = w.astype(BF16)
        outs[2][...] = w
    acc = jnp.dot(u_scr[...], w, preferred_element_type=F32)

    @pl.when(j < N_LRU_BLOCKS)
    def _():
        zl_ref[...] = acc

    @pl.when(j >= N_LRU_BLOCKS)
    def _():
        zr_ref[...] = acc.astype(BF16)


def _inproj(x, g, w, *, nb, seq_len, bt, cast=()):
    assert bt == 1 or bt % PERM_T == 0
    m, bm, bn = nb * seq_len, nb * bt, INPROJ_BN
    emit_w = w.dtype == F32
    assert not emit_w or m == bm
    w_spec = pl.BlockSpec((D_MODEL, bn), lambda i, j: (0, j))
    perm = _perm_matrix(nb, True) if bt > 1 else jnp.zeros((SUBLANES, LANES), BF16)
    cast_specs, cast_shapes = _side_cast_specs(cast, m // bm, lambda i, j: i)
    return pl.pallas_call(
        functools.partial(_inproj_kernel, nb=nb, bt=bt, n_cast=len(cast), emit_w=emit_w),
        grid=(m // bm, IN_WIDTH // bn),
        in_specs=[_row_tile_spec(nb, bt, D_MODEL),
                  pl.BlockSpec((1, D_MODEL), lambda i, j: (0, 0)),
                  _const_spec(perm.shape),
                  w_spec] + cast_specs,
        out_specs=[pl.BlockSpec((bm, bn), lambda i, j: (i, jnp.minimum(j, N_LRU_BLOCKS - 1))),
                   pl.BlockSpec((bm, bn), lambda i, j: (i, _zr_block(j)))]
        + [w_spec] * int(emit_w) + cast_specs,
        out_shape=[jax.ShapeDtypeStruct((m, D_LRU), F32), jax.ShapeDtypeStruct((m, ZR_WIDTH), BF16)]
        + [jax.ShapeDtypeStruct(w.shape, BF16)] * int(emit_w) + cast_shapes,
        scratch_shapes=[pltpu.VMEM((bm, D_MODEL), BF16)],
        compiler_params=_params(("parallel", "arbitrary")),
        name="inproj",
    )(x, g, perm, w, *cast)


def _lru_kernel(*refs, nb, bt, n_tiles, first_pos_is_zero, n_cast):
    ((x_ref, conv0_ref, h0_ref, cw_ref, cb_ref, wax_ref, bax_ref, lam_ref, proj_ref),
     cast_src, (ya_ref, conv_ref, ht_ref), cast_dst,
     (xe_scr, a_scr, b_scr, hq_scr)) = _split_refs(refs, 9, 3, n_cast)
    c = pl.program_id(0)
    rows = nb * bt
    pad = 2 * SUBLANES
    pipelined = bt > 1

    @pl.when(c == 0)
    def _():
        conv_ref[...] = conv0_ref[...]
        ht_ref[...] = h0_ref[...]
        if pipelined:
            xe_scr[:pad - 3 * nb, :] = jnp.zeros((pad - 3 * nb, D_LRU), F32)
            hq_scr[...] = jnp.zeros(hq_scr.shape, BF16)

    nlam = -lam_ref[...]
    sp = jnp.maximum(nlam, 0.0) + jnp.log1p(jnp.exp(-jnp.abs(nlam)))
    k_tanh = LRU_C * sp
    k_exp2 = (-LRU_C * LOG2E) * sp

    if bt > 1:
        for k in range(CONV_W - 1):
            xe_scr[pad - (3 - k) * nb:pad - (2 - k) * nb, :] = conv_ref[:, k, :]
        xe_scr[pad:, :] = x_ref[...]
        is_first = (lax.broadcasted_iota(jnp.int32, (rows, LRU_BW), 0) < nb) & (c == 0)

    def project_prev(q):
        qs = slice(q * LRU_BW, (q + 1) * LRU_BW)
        ya = jnp.dot(hq_scr[...], proj_ref[:, qs], preferred_element_type=F32)
        ya_ref[:, qs] = ya.astype(BF16)

    lo = _lo_rows((SUBLANES, LANES))
    h_last = {}

    def scan_columns(col_blocks):
        cols = {cb: slice(cb * LANES, (cb + 1) * LANES) for cb in col_blocks}
        hs = {cb: jnp.concatenate([ht_ref[:, cl], ht_ref[:, cl]], axis=0) for cb, cl in cols.items()}
        for k in range(rows // SUBLANES):
            rs = slice(k * SUBLANES, (k + 1) * SUBLANES)
            for cb, cl in cols.items():
                av, bv = a_scr[rs, cl], b_scr[rs, cl]
                first = av * hs[cb] + bv
                second = av * pltpu.roll(first, HALF, axis=0) + bv
                h = jnp.where(lo, first, second)
                b_scr[rs, cl] = h
                hs[cb] = pltpu.roll(h, HALF, axis=0)
        h_last.update(hs)

    for hb in range(H_LRU):
        cs = slice(hb * LRU_BW, (hb + 1) * LRU_BW)
        if pipelined:
            project_prev(hb)
        if hb < n_cast:
            cast_dst[hb][...] = cast_src[hb][...].astype(BF16)
        if bt > 1:
            xs = pltpu.roll(xe_scr[:, cs], nb, axis=0)
            xc = (cb_ref[:, cs] + xs[pad - 2 * nb:pad + rows - 2 * nb] * cw_ref[0:1, cs]
                  + xe_scr[pad - 2 * nb:pad + rows - 2 * nb, cs] * cw_ref[1:2, cs]
                  + xs[pad:] * cw_ref[2:3, cs]
                  + xe_scr[pad:, cs] * cw_ref[3:4, cs])
        else:
            xc = (cb_ref[:, cs] + conv_ref[:, 0, cs] * cw_ref[0:1, cs]
                  + conv_ref[:, 1, cs] * cw_ref[1:2, cs] + conv_ref[:, 2, cs] * cw_ref[2:3, cs]
                  + x_ref[:, cs] * cw_ref[3:4, cs])
        gates = jnp.dot(xc.astype(BF16), wax_ref[hb], preferred_element_type=F32) + bax_ref[hb]
        r = _sigmoid(gates[:, :LRU_BW])
        i = _sigmoid(gates[:, LRU_BW:])
        a = jnp.exp2(r * k_exp2[:, cs])
        m2 = jnp.tanh(r * k_tanh[:, cs]) * (1.0 + a * a)
        mult = jnp.where(m2 > 0.0, m2 * lax.rsqrt(m2), 0.0)
        if first_pos_is_zero:
            mult = jnp.where(is_first, 1.0, mult)
        a_scr[:, cs] = a
        b_scr[:, cs] = mult * (i * xc)
        if pipelined:
            scan_columns(range(hb * LRU_BW // LANES, (hb + 1) * LRU_BW // LANES))

    if not pipelined:
        conv_ref[:, 0, :] = conv_ref[:, 1, :]
        conv_ref[:, 1, :] = conv_ref[:, 2, :]
        conv_ref[:, 2, :] = x_ref[...]
        h = a_scr[...] * ht_ref[...] + b_scr[...]
        ht_ref[...] = h
        ya = jnp.dot(h.astype(BF16), proj_ref[...], preferred_element_type=F32)
        ya_ref[...] = ya.astype(BF16)
        return

    @pl.when(c < n_tiles)
    def _():
        for k in range(CONV_W - 1):
            conv_ref[:, k, :] = x_ref[rows - (3 - k) * nb:rows - (2 - k) * nb, :]
        for cb, h in h_last.items():
            ht_ref[:, cb * LANES:(cb + 1) * LANES] = h[:HALF]
        hq_scr[...] = b_scr[...].astype(BF16)


def _lru(z, conv0, h0, cw, cb, wax, bax, lam, proj, *, nb, seq_len, bt, first_pos_is_zero, cast=()):
    assert bt == 1 or (nb == HALF and bt % 2 == 0 and bt >= 4)
    assert len(cast) <= H_LRU
    rows, n_tiles = nb * bt, seq_len // bt
    pipelined = bt > 1
    st = pl.BlockSpec((nb, D_LRU), lambda c: (0, 0))
    st_sds = jax.ShapeDtypeStruct((nb, D_LRU), F32)
    cst = pl.BlockSpec((nb, CONV_W - 1, D_LRU), lambda c: (0, 0, 0))
    cst_sds = jax.ShapeDtypeStruct((nb, CONV_W - 1, D_LRU), F32)
    x_tile = pl.BlockSpec((rows, D_LRU), lambda c: (jnp.minimum(c, n_tiles - 1), 0))
    ya_tile = pl.BlockSpec((rows, D_LRU), lambda c: (jnp.maximum(c - int(pipelined), 0), 0))
    kern = functools.partial(_lru_kernel, nb=nb, bt=bt, n_tiles=n_tiles,
                             first_pos_is_zero=first_pos_is_zero, n_cast=len(cast))
    cast_specs, cast_shapes = _side_cast_specs(cast, n_tiles, lambda c: jnp.minimum(c, n_tiles - 1))
    small = (SUBLANES, LANES)
    ya, conv, ht, *cast_out = pl.pallas_call(
        kern,
        grid=(n_tiles + int(pipelined),),
        in_specs=[x_tile, cst, st,
                  _const_spec((CONV_W, D_LRU)), _const_spec((1, D_LRU)),
                  _const_spec((H_LRU, LRU_BW, 2 * LRU_BW)), _const_spec((H_LRU, 1, 2 * LRU_BW)),
                  _const_spec((1, D_LRU)), _const_spec((D_LRU, D_MODEL))] + cast_specs,
        out_specs=[ya_tile, cst, st] + cast_specs,
        out_shape=[jax.ShapeDtypeStruct((nb * seq_len, D_LRU), BF16), cst_sds, st_sds] + cast_shapes,
        scratch_shapes=[pltpu.VMEM((rows + 2 * SUBLANES, D_LRU) if pipelined else small, F32),
                        pltpu.VMEM((rows, D_LRU), F32), pltpu.VMEM((rows, D_LRU), F32),
                        pltpu.VMEM((rows, D_LRU) if pipelined else (2 * SUBLANES, LANES), BF16)],
        compiler_params=_params(("arbitrary",)),
        name="lru",
    )(z, conv0, h0, cw, cb, wax, bax, lam, proj, *cast)
    return ya, conv, ht, cast_out


def _gelu_tanh(x):
    c = math.sqrt(2.0 / math.pi)
    return 0.5 * x * (1.0 + jnp.tanh(c * (x + 0.044715 * (x * x * x))))


S5_GROUPS = 4
S5_GROUP_SLABS = S5_SLABS // S5_GROUPS
S5_GROUP_COLS = S5_N // S5_GROUPS
GLU_GROUP_COLS = D_MODEL // S5_GROUPS


def _s5_b_proj(ub, up, wb_ref, xr_scr, xi_scr, j):
    js = slice(j * S5_SLAB_IN, (j + 1) * S5_SLAB_IN)
    if up is None:
        w = jnp.dot(ub[:, js], wb_ref[j, :S5_SLAB_IN, :], preferred_element_type=F32)
    else:
        w = jnp.dot(jnp.concatenate([ub[:, js], up[:, js]], axis=1), wb_ref[j],
                    preferred_element_type=F32)
    xr_scr[:, j * S5_SLAB_ST:(j + 1) * S5_SLAB_ST] = w[:, :S5_SLAB_ST]
    xi_scr[:, j * S5_SLAB_ST:(j + 1) * S5_SLAB_ST] = w[:, S5_SLAB_ST:]


def _s5_c_proj(xr_scr, xi_scr, wc_ref, y_scr, j):
    ss = slice(j * S5_SLAB_ST, (j + 1) * S5_SLAB_ST)
    xs = jnp.concatenate([xr_scr[:, ss].astype(BF16), xi_scr[:, ss].astype(BF16)], axis=1)
    y_scr[:, j * S5_SLAB_IN:(j + 1) * S5_SLAB_IN] = jnp.dot(xs, wc_ref[j], preferred_element_type=F32)


def _s5_kernel(*refs, nb, bt, n_tiles, n_cast):
    ((u_ref, s0r_ref, s0i_ref, car_ref, cai_ref, wb_ref, wc_ref, d_ref, wv_ref, wg_ref), cast_src,
     (yb_ref, sr_ref, si_ref), cast_dst,
     (xr_scr, xi_scr, pr_scr, pi_scr, y_scr, v_scr, ulast_scr)) = _split_refs(refs, 10, 3, n_cast)
    c = pl.program_id(0)
    rows = nb * bt
    ub = u_ref[...]
    u = ub.astype(F32)

    if bt == 1:
        for j in range(S5_SLABS):
            _s5_b_proj(ub, None, wb_ref, xr_scr, xi_scr, j)
        ar, ai = car_ref[0:1, :], cai_ref[0:1, :]
        pr, pi = s0r_ref[...], s0i_ref[...]
        nr = ar * pr - ai * pi + xr_scr[...]
        ni = ar * pi + ai * pr + xi_scr[...]
        xr_scr[...] = nr
        xi_scr[...] = ni
        sr_ref[...] = nr
        si_ref[...] = ni
        for j in range(S5_SLABS):
            _s5_c_proj(xr_scr, xi_scr, wc_ref, y_scr, j)
        v = _gelu_tanh(y_scr[...] + d_ref[...] * u).astype(BF16)
        yv = jnp.dot(v, wv_ref[...], preferred_element_type=F32)
        yg = jnp.dot(v, wg_ref[...], preferred_element_type=F32)
        yb_ref[...] = (yv * _sigmoid(yg)).astype(BF16)
        return

    @pl.when(c == 0)
    def _():
        pr_scr[...] = jnp.zeros(pr_scr.shape, F32)
        pi_scr[...] = jnp.zeros(pi_scr.shape, F32)
        ulast_scr[...] = jnp.zeros(ulast_scr.shape, F32)
        v_scr[...] = jnp.zeros(v_scr.shape, BF16)

    up = pltpu.roll(jnp.concatenate([ulast_scr[...], u], axis=0), HALF, axis=0)[SUBLANES:]
    up = up.astype(BF16)
    ulast_scr[...] = u[rows - SUBLANES:, :]
    lo = _lo_rows((SUBLANES, LANES))
    first = c == 0
    v_prev = v_scr[...]
    for grp in range(S5_GROUPS):
        gs = slice(grp * GLU_GROUP_COLS, (grp + 1) * GLU_GROUP_COLS)
        yv = jnp.dot(v_prev, wv_ref[:, gs], preferred_element_type=F32)
        yg = jnp.dot(v_prev, wg_ref[:, gs], preferred_element_type=F32)
        yb_ref[:, gs] = (yv * _sigmoid(yg)).astype(BF16)
        if grp < n_cast:
            cast_dst[grp][...] = cast_src[grp][...].astype(BF16)

        slabs = range(grp * S5_GROUP_SLABS, (grp + 1) * S5_GROUP_SLABS)
        for j in slabs:
            _s5_b_proj(ub, up, wb_ref, xr_scr, xi_scr, j)
        cols = [slice(grp * S5_GROUP_COLS + cb * LANES, grp * S5_GROUP_COLS + (cb + 1) * LANES)
                for cb in range(S5_GROUP_COLS // LANES)]
        carry, coeff = [], []
        for cl in cols:
            ar, ai = car_ref[:, cl], cai_ref[:, cl]
            s0r = jnp.concatenate([s0r_ref[:, cl], s0r_ref[:, cl]], axis=0)
            s0i = jnp.concatenate([s0i_ref[:, cl], s0i_ref[:, cl]], axis=0)
            xr_scr[:SUBLANES, cl] += jnp.where(first, ar * s0r - ai * s0i, 0.0)
            xi_scr[:SUBLANES, cl] += jnp.where(first, ar * s0i + ai * s0r, 0.0)
            coeff.append((jnp.where(lo, pltpu.roll(ar, HALF, axis=0), ar),
                          jnp.where(lo, pltpu.roll(ai, HALF, axis=0), ai)))
            carry.append((pr_scr[:, cl], pi_scr[:, cl]))
        for k in range(rows // SUBLANES):
            rs = slice(k * SUBLANES, (k + 1) * SUBLANES)
            for cb, cl in enumerate(cols):
                (pr, pi), (ar, ai) = carry[cb], coeff[cb]
                nr = ar * pr - ai * pi + xr_scr[rs, cl]
                ni = ar * pi + ai * pr + xi_scr[rs, cl]
                xr_scr[rs, cl] = nr
                xi_scr[rs, cl] = ni
                carry[cb] = (nr, ni)
        for cb, cl in enumerate(cols):
            pr_scr[:, cl], pi_scr[:, cl] = carry[cb]
        for j in slabs:
            _s5_c_proj(xr_scr, xi_scr, wc_ref, y_scr, j)
    v_scr[...] = _gelu_tanh(y_scr[...] + d_ref[...] * u).astype(BF16)

    @pl.when(c == n_tiles - 1)
    def _():
        sr_ref[...] = pr_scr[HALF:, :]
        si_ref[...] = pi_scr[HALF:, :]


def _s5(z, s0r, s0i, ca_re, ca_im, wb, wc, d, wv, wg, *, nb, seq_len, bt, cast=()):
    assert bt == 1 or (nb == HALF and bt % 2 == 0)
    assert len(cast) <= S5_GROUPS
    rows, n_tiles = nb * bt, seq_len // bt
    pipelined = bt > 1
    st = pl.BlockSpec((nb, S5_N), lambda c: (0, 0))
    st_sds = jax.ShapeDtypeStruct((nb, S5_N), F32)
    x_s5_block = (ZR_WIDTH - D_S5) // D_S5
    cast_specs, cast_shapes = _side_cast_specs(cast, n_tiles, lambda c: jnp.minimum(c, n_tiles - 1))
    yb, sr, si, *cast_out = pl.pallas_call(
        functools.partial(_s5_kernel, nb=nb, bt=bt, n_tiles=n_tiles, n_cast=len(cast)),
        grid=(n_tiles + int(pipelined),),
        in_specs=[pl.BlockSpec((rows, D_S5), lambda c: (jnp.minimum(c, n_tiles - 1), x_s5_block)),
                  st, st, _const_spec((SUBLANES, S5_N)), _const_spec((SUBLANES, S5_N)),
                  _const_spec((S5_SLABS, 2 * S5_SLAB_IN, 2 * S5_SLAB_ST)),
                  _const_spec((S5_SLABS, 2 * S5_SLAB_ST, S5_SLAB_IN)),
                  _const_spec((1, D_S5)), _const_spec((D_S5, D_MODEL)),
                  _const_spec((D_S5, D_MODEL))] + cast_specs,
        out_specs=[pl.BlockSpec((rows, D_MODEL), lambda c: (jnp.maximum(c - int(pipelined), 0), 0)),
                   st, st] + cast_specs,
        out_shape=[jax.ShapeDtypeStruct((nb * seq_len, D_MODEL), BF16), st_sds, st_sds] + cast_shapes,
        scratch_shapes=[pltpu.VMEM((rows, S5_N), F32), pltpu.VMEM((rows, S5_N), F32),
                        pltpu.VMEM((SUBLANES, S5_N), F32), pltpu.VMEM((SUBLANES, S5_N), F32),
                        pltpu.VMEM((rows, D_S5), F32), pltpu.VMEM((rows, D_S5), BF16),
                        pltpu.VMEM((SUBLANES, D_S5), F32)],
        compiler_params=_params(("arbitrary",)),
        name="s5",
    )(z, s0r, s0i, ca_re, ca_im, wb, wc, d, wv, wg, *cast)
    return yb, sr, si, cast_out


def _merge_kernel(ya_ref, yb_ref, ga_ref, gb_ref, x_ref, perm_ref, w_ref, g_ref, x1_ref, u2_ref,
                  *, nb, bt):
    def gated(rs):
        f32 = lambda ref: ref[rs, :].astype(F32)
        merged = _sigmoid(f32(ga_ref)) * f32(ya_ref) + _sigmoid(f32(gb_ref)) * f32(yb_ref)
        return merged.astype(BF16)

    if bt == 1:
        x1 = x_ref[...] + jnp.dot(gated(slice(None)), w_ref[...], preferred_element_type=F32)
        x1_ref[...] = x1
        u2_ref[...] = _rmsnorm(x1, g_ref[...]).astype(BF16)
        return
    n_sub = bt // PERM_T
    sub_rows = lambda s: slice(s * nb * PERM_T, (s + 1) * nb * PERM_T)

    def finish(s, delta):
        ts = slice(s * PERM_T, (s + 1) * PERM_T)
        for b in range(nb):
            x1 = x_ref[b, ts, :] + delta[b * PERM_T:(b + 1) * PERM_T]
            x1_ref[b, ts, :] = x1
            u2_ref[b, ts, :] = _rmsnorm(x1, g_ref[...]).astype(BF16)

    merged = gated(sub_rows(0))
    pending = None
    for s in range(n_sub):
        mb = jnp.dot(perm_ref[...], merged, preferred_element_type=F32).astype(BF16)
        delta = jnp.dot(mb, w_ref[...], preferred_element_type=F32)
        if s + 1 < n_sub:
            merged = gated(sub_rows(s + 1))
        if pending is not None:
            finish(*pending)
        pending = (s, delta)
    finish(*pending)


def _merge(ya, yb, zr, x, w_out, g, *, nb, seq_len, bt):
    assert bt == 1 or bt % PERM_T == 0
    bm = nb * bt
    row = lambda col: pl.BlockSpec((bm, D_MODEL), lambda i: (i, col))
    perm = _perm_matrix(nb, False) if bt > 1 else jnp.zeros((SUBLANES, LANES), BF16)
    tile = _row_tile_spec(nb, bt, D_MODEL)
    return pl.pallas_call(
        functools.partial(_merge_kernel, nb=nb, bt=bt),
        grid=(seq_len // bt,),
        in_specs=[row(0), row(0), row(0), row(1), tile, _const_spec(perm.shape),
                  _const_spec((D_MODEL, D_MODEL)), _const_spec((1, D_MODEL))],
        out_specs=(tile, tile),
        out_shape=(jax.ShapeDtypeStruct(x.shape, F32), jax.ShapeDtypeStruct(x.shape, BF16)),
        compiler_params=_params(("parallel",)),
        name="merge",
    )(ya, yb, zr, zr, x, perm, w_out, g)


def _ffn_kernel(u_ref, x1_ref, ue_ref, x1e_ref, wg_ref, wu_ref, wd_ref, g_ref, y_ref, ye_ref,
                acc_scr, acce_scr):
    i, f = pl.program_id(0), pl.program_id(1)
    first, last = f == 0, f == pl.num_programs(1) - 1
    with_extra = i == 0
    bm = u_ref.shape[0]

    def down_proj(u):
        hg = jnp.dot(u, wg_ref[...], preferred_element_type=F32)
        hu = jnp.dot(u, wu_ref[...], preferred_element_type=F32)
        hid = (hg * _sigmoid(hg) * hu).astype(BF16)
        return jnp.dot(hid, wd_ref[...], preferred_element_type=F32)

    @pl.when(first)
    def _():
        acc_scr[...] = x1_ref[...]

    @pl.when(first & with_extra)
    def _():
        acce_scr[...] = x1e_ref[...]

    @pl.when(with_extra)
    def _():
        down = down_proj(jnp.concatenate([u_ref[...], ue_ref[...]], axis=0))
        acc_scr[...] += down[:bm]
        acce_scr[...] += down[bm:]

    @pl.when(jnp.logical_not(with_extra))
    def _():
        acc_scr[...] += down_proj(u_ref[...])

    @pl.when(last)
    def _():
        y_ref[...] = _rmsnorm(acc_scr[...], g_ref[...])

    @pl.when(last & with_extra)
    def _():
        ye_ref[...] = _rmsnorm(acce_scr[...], g_ref[...])


def _ffn(u2, x1, u2_extra, x1_extra, wg, wu, wd, g, bm, bf):
    m, me = x1.shape[0], x1_extra.shape[0]
    row = pl.BlockSpec((bm, D_MODEL), lambda i, f: (i, 0))
    extra = pl.BlockSpec((me, D_MODEL), lambda i, f: (0, 0))
    return pl.pallas_call(
        _ffn_kernel,
        grid=(m // bm, D_FF // bf),
        in_specs=[row, row, extra, extra,
                  pl.BlockSpec((D_MODEL, bf), lambda i, f: (0, f)),
                  pl.BlockSpec((D_MODEL, bf), lambda i, f: (0, f)),
                  pl.BlockSpec((bf, D_MODEL), lambda i, f: (f, 0)),
                  pl.BlockSpec((1, D_MODEL), lambda i, f: (0, 0))],
        out_specs=(row, extra),
        out_shape=(jax.ShapeDtypeStruct((m, D_MODEL), F32), jax.ShapeDtypeStruct((me, D_MODEL), F32)),
        scratch_shapes=[pltpu.VMEM((bm, D_MODEL), F32), pltpu.VMEM((me, D_MODEL), F32)],
        compiler_params=_params(("arbitrary", "arbitrary")),
        name="ffn",
    )(u2, x1, u2_extra, x1_extra, wg, wu, wd, g)


def _mixer(x, conv0, h0, s0r, s0i, w, w32, *, bt_in, bt_lru, bt_s5, bt_merge, first_pos_is_zero,
           z=None):
    nb, seq_len = (x.shape[0], 1) if x.ndim == 2 else x.shape[:2]
    seq = dict(nb=nb, seq_len=seq_len)
    w = dict(w)
    take = lambda *names: tuple(w32[n] for n in names) if w32 else ()
    keep = lambda names, arrs: w.update(zip(names, arrs)) if w32 else None

    if z is None:
        zl, zr, *cast = _inproj(x, w['norm_mix_g'], w['w_in'], bt=bt_in, cast=take('proj'), **seq)
        keep(('proj',), cast)
    else:
        zl, zr = z
    ya, conv_n, h_n, cast = _lru(zl, conv0, h0, w['cw'], w['cb'], w['wax'], w['bax'], w['lam'],
                                 w['proj'], bt=bt_lru, first_pos_is_zero=first_pos_is_zero,
                                 cast=take('w_out', 'wv', 'wg'), **seq)
    keep(('w_out', 'wv', 'wg'), cast)
    yb, sr, si, cast = _s5(zr, s0r, s0i, w['ca_re'], w['ca_im'], w['wb'], w['wc'], w['d'], w['wv'],
                           w['wg'], bt=bt_s5, cast=take('ffn_wg', 'ffn_wu', 'ffn_wd'), **seq)
    keep(('ffn_wg', 'ffn_wu', 'ffn_wd'), cast)
    x1, u2 = _merge(ya, yb, zr, x, w['w_out'], w['norm_ffn_g'], bt=bt_merge, **seq)
    m = nb * seq_len
    return (x1.reshape(m, D_MODEL), u2.reshape(m, D_MODEL), conv_n, h_n, sr, si), w


def kernel(x_prompt, x_sample, state_lru_conv, state_lru_h, state_s5_re, state_s5_im, norm_mix_g, w_in, lru_conv_w, lru_conv_b, lru_wa, lru_ba, lru_wx, lru_bx, lru_lambda, lru_proj, s5_lambda_re, s5_lambda_im, s5_log_dt, s5_b_re, s5_b_im, s5_c_re, s5_c_im, s5_d, s5_glu_wv, s5_glu_wg, w_out, norm_ffn_g, ffn_w_gate, ffn_w_up, ffn_w_down, norm_final_g):
    batch, seq, _ = x_prompt.shape
    dec_batch = x_sample.shape[0]
    l = 0
    ca_re, ca_im, wb, wc = _s5_prepare(s5_lambda_re[l], s5_lambda_im[l], s5_log_dt[l],
                                       s5_b_re[l], s5_b_im[l], s5_c_re[l], s5_c_im[l])
    w = dict(
        norm_mix_g=norm_mix_g[l][None, :],
        cw=lru_conv_w[l], cb=lru_conv_b[l][None, :],
        wax=jnp.concatenate([lru_wa[l], lru_wx[l]], axis=-1).astype(BF16),
        bax=jnp.concatenate([lru_ba[l], lru_bx[l]], axis=-1)[:, None, :],
        lam=lru_lambda[l][None, :],
        ca_re=ca_re, ca_im=ca_im, wb=wb, wc=wc, d=s5_d[l].reshape(1, D_S5),
        norm_ffn_g=norm_ffn_g[l][None, :], norm_final_g=norm_final_g[None, :],
    )
    w32 = dict(proj=lru_proj[l], w_out=w_out[l], wv=s5_glu_wv[l], wg=s5_glu_wg[l],
               ffn_wg=ffn_w_gate[l], ffn_wu=ffn_w_up[l], ffn_wd=ffn_w_down[l])
    zeros = lambda *s: jnp.zeros(s, F32)
    xs = x_sample.reshape(dec_batch, D_MODEL)
    zl_s, zr_s, w['w_in'] = _inproj(xs, w['norm_mix_g'], w_in[l], nb=dec_batch, seq_len=1, bt=1)
    (x1_p, u2_p, conv_p, h_p, sr_p, si_p), w = _mixer(
        x_prompt, zeros(batch, CONV_W - 1, D_LRU), zeros(batch, D_LRU),
        zeros(batch, S5_N), zeros(batch, S5_N), w, w32,
        bt_in=256, bt_lru=128, bt_s5=64, bt_merge=128, first_pos_is_zero=True)
    (x1_s, u2_s, conv_s, h_s, sr_s, si_s), _ = _mixer(
        xs, state_lru_conv[l], state_lru_h[l],
        state_s5_re[l].reshape(dec_batch, S5_N), state_s5_im[l].reshape(dec_batch, S5_N), w, None,
        bt_in=1, bt_lru=1, bt_s5=1, bt_merge=1, first_pos_is_zero=False, z=(zl_s, zr_s))
    yp, ys = _ffn(u2_p, x1_p, u2_s, x1_s, w['ffn_wg'], w['ffn_wu'], w['ffn_wd'], w['norm_final_g'],
                  bm=512, bf=512)
    st = lambda a, n: a.reshape(1, n, S5_G, S5_P)
    return (yp.reshape(batch, seq, D_MODEL), ys.reshape(dec_batch, 1, D_MODEL),
            conv_p[None], h_p[None], st(sr_p, batch), st(si_p, batch),
            conv_s[None], h_s[None], st(sr_s, dec_batch), st(si_s, dec_batch))
```

```python
import functools
import math

import jax
import jax.numpy as jnp
import numpy as np
from jax import lax
from jax.experimental import pallas as pl
from jax.experimental.pallas import tpu as pltpu

F32 = jnp.float32
BF16 = jnp.bfloat16

D_MODEL = 2048
D_LRU = D_MODEL
H_LRU = 8
LRU_BW = D_LRU // H_LRU
LRU_C = 8.0
CONV_W = 4
D_S5 = D_MODEL // 2
S5_H = 16
S5_G = D_S5 // S5_H
S5_P = 64
S5_N = S5_G * S5_P
D_FF = 5632
IN_WIDTH = D_LRU + D_S5 + 2 * D_MODEL
EPS = 1e-6

S5_SLAB_G = 8
S5_SLABS = S5_G // S5_SLAB_G
S5_SLAB_IN = S5_SLAB_G * S5_H
S5_SLAB_ST = S5_SLAB_G * S5_P

LANES = 128
SUBLANES = 8
HALF = SUBLANES // 2
VMEM_LIMIT_BYTES = 56 * 1024 * 1024


def _params(sem):
    return pltpu.CompilerParams(dimension_semantics=sem, vmem_limit_bytes=VMEM_LIMIT_BYTES)


def _const_spec(shape):
    nd = len(shape)
    return pl.BlockSpec(shape, lambda *_: (0,) * nd, pipeline_mode=pl.Buffered(1))


def _side_cast_specs(arrs, n_steps, step_of):
    specs, shapes = [], []
    for a in arrs:
        rows = a.shape[0] // n_steps
        assert rows * n_steps == a.shape[0] and rows % (2 * SUBLANES) == 0
        specs.append(pl.BlockSpec((rows, a.shape[1]), lambda *g: (step_of(*g), 0)))
        shapes.append(jax.ShapeDtypeStruct(a.shape, BF16))
    return specs, shapes


def _split_refs(refs, n_in, n_out, n_cast):
    cuts = [n_in, n_in + n_cast, n_in + n_cast + n_out, n_in + 2 * n_cast + n_out]
    parts = [refs[a:b] for a, b in zip([0] + cuts, cuts + [len(refs)])]
    return parts


def _rmsnorm(x, g):
    ms = jnp.mean(x * x, axis=-1, keepdims=True)
    return x * lax.rsqrt(ms + EPS) * g


LOG2E = 1.4426950408889634


def _sigmoid(x):
    return 1.0 / (1.0 + jnp.exp2(x * (-LOG2E)))


def _lo_rows(shape):
    return lax.broadcasted_iota(jnp.int32, shape, 0) % SUBLANES < HALF


def _s5prep_kernel(lre_ref, lim_ref, ldt_ref, br_ref, bi_ref,
                   bbr_ref, bbi_ref, abbr_ref, abbi_ref, abr_ref, abi_ref, a2r_ref, a2i_ref):
    lre = lre_ref[...]
    lim = lim_ref[...]
    dt = jnp.exp(ldt_ref[...])
    mag = jnp.exp(lre * dt)
    ab_re = mag * jnp.cos(lim * dt)
    ab_im = mag * jnp.sin(lim * dt)
    e_re = ab_re - 1.0
    e_im = ab_im
    den = lre * lre + lim * lim
    co_re = (e_re * lre + e_im * lim) / den
    co_im = (e_im * lre - e_re * lim) / den
    br = br_ref[...]
    bi = bi_ref[...]
    bb_re = co_re * br - co_im * bi
    bb_im = co_re * bi + co_im * br
    bbr_ref[...] = bb_re
    bbi_ref[...] = bb_im
    abbr_ref[...] = ab_re * bb_re - ab_im * bb_im
    abbi_ref[...] = ab_re * bb_im + ab_im * bb_re
    abr_ref[...] = ab_re
    abi_ref[...] = ab_im
    a2r_ref[...] = ab_re * ab_re - ab_im * ab_im
    a2i_ref[...] = 2.0 * (ab_re * ab_im)


def _s5_prepare(lam_re, lam_im, log_dt, b_re, b_im, c_re, c_im):
    full = (S5_G, S5_P, S5_H)
    flat = (S5_G * S5_P * S5_H // LANES, LANES)
    bc = lambda a: jnp.broadcast_to(a, full).reshape(flat)
    args = (bc(lam_re[:, :, None]), bc(lam_im[:, :, None]), bc(log_dt[:, None, None]),
            b_re.reshape(flat), b_im.reshape(flat))
    sds = jax.ShapeDtypeStruct(flat, F32)
    bbr, bbi, abbr, abbi, abr, abi, a2r, a2i = pl.pallas_call(
        _s5prep_kernel, out_shape=(sds,) * 8, name="s5prep")(*args)
    per_state = lambda a: a.reshape(full)[:, :, 0].reshape(1, S5_N)
    coeff = lambda a1, a2: jnp.concatenate(
        [jnp.broadcast_to(per_state(a1), (HALF, S5_N)), jnp.broadcast_to(per_state(a2), (HALF, S5_N))], 0)
    ca_re, ca_im = coeff(abr, a2r), coeff(abi, a2i)
    eye = np.eye(S5_SLAB_G, dtype=np.float32)

    def slab_b(re, im):
        b = jnp.stack([re.reshape(full), im.reshape(full)], 0)
        b = b.reshape(2, S5_SLABS, S5_SLAB_G, S5_P, S5_H)
        return jnp.einsum('cjgph,gk->jghckp', b, eye).reshape(S5_SLABS, S5_SLAB_IN, 2 * S5_SLAB_ST)

    wb = jnp.concatenate([slab_b(bbr, bbi), slab_b(abbr, abbi)], axis=1)
    cc = jnp.stack([c_re, -c_im], 0).reshape(2, S5_SLABS, S5_SLAB_G, S5_H, S5_P)
    wc = jnp.einsum('cjghp,gk->jcgpkh', cc, eye).reshape(S5_SLABS, 2 * S5_SLAB_ST, S5_SLAB_IN)
    return ca_re, ca_im, wb.astype(BF16), wc.astype(BF16)


PERM_T = 64


def _perm_matrix(nb, to_time_major):
    n = nb * PERM_T
    tm = np.arange(n)
    sm = (tm % nb) * PERM_T + tm // nb
    p = np.zeros((n, n), np.float32)
    p[tm, sm] = 1.0
    return jnp.asarray(p if to_time_major else p.T, dtype=BF16)


def _row_tile_spec(nb, bt, width):
    if bt == 1:
        return pl.BlockSpec((nb, width), lambda *g: (0, 0))
    return pl.BlockSpec((nb, bt, width), lambda *g: (0, g[0], 0))


N_LRU_BLOCKS = 2
INPROJ_BN = 1024
ZR_WIDTH = IN_WIDTH - D_LRU


def _zr_block(j):
    return jnp.where(j <= N_LRU_BLOCKS, (ZR_WIDTH - D_S5) // INPROJ_BN, j - N_LRU_BLOCKS - 1)


def _inproj_kernel(*refs, nb, bt, n_cast, emit_w):
    (x_ref, g_ref, perm_ref, w_ref), cast_src, outs, cast_dst, (u_scr,) = _split_refs(
        refs, 4, 2 + int(emit_w), n_cast)
    zl_ref, zr_ref = outs[:2]
    j = pl.program_id(1)

    @pl.when(j == 0)
    def _():
        for src, dst in zip(cast_src, cast_dst):
            dst[...] = src[...].astype(BF16)
        if bt == 1:
            u_scr[...] = _rmsnorm(x_ref[...], g_ref[...]).astype(BF16)
        else:
            for s in range(bt // PERM_T):
                ts = slice(s * PERM_T, (s + 1) * PERM_T)
                xin = jnp.concatenate([x_ref[b, ts, :] for b in range(nb)], axis=0)
                u = _rmsnorm(xin, g_ref[...]).astype(BF16)
                u_scr[s * nb * PERM_T:(s + 1) * nb * PERM_T, :] = jnp.dot(
                    perm_ref[...], u, preferred_element_type=F32).astype(BF16)

    w = w_ref[...]
    if emit_w:
        w = w.astype(BF16)
        outs[2][...] = w
    acc = jnp.dot(u_scr[...], w, preferred_element_type=F32)

    @pl.when(j < N_LRU_BLOCKS)
    def _():
        zl_ref[...] = acc

    @pl.when(j >= N_LRU_BLOCKS)
    def _():
        zr_ref[...] = acc.astype(BF16)


def _inproj(x, g, w, *, nb, seq_len, bt, cast=()):
    assert bt == 1 or bt % PERM_T == 0
    m, bm, bn = nb * seq_len, nb * bt, INPROJ_BN
    emit_w = w.dtype == F32
    assert not emit_w or m == bm
    w_spec = pl.BlockSpec((D_MODEL, bn), lambda i, j: (0, j))
    perm = _perm_matrix(nb, True) if bt > 1 else jnp.zeros((SUBLANES, LANES), BF16)
    cast_specs, cast_shapes = _side_cast_specs(cast, m // bm, lambda i, j: i)
    return pl.pallas_call(
        functools.partial(_inproj_kernel, nb=nb, bt=bt, n_cast=len(cast), emit_w=emit_w),
        grid=(m // bm, IN_WIDTH // bn),
        in_specs=[_row_tile_spec(nb, bt, D_MODEL),
                  pl.BlockSpec((1, D_MODEL), lambda i, j: (0, 0)),
                  _const_spec(perm.shape),
                  w_spec] + cast_specs,
        out_specs=[pl.BlockSpec((bm, bn), lambda i, j: (i, jnp.minimum(j, N_LRU_BLOCKS - 1))),
                   pl.BlockSpec((bm, bn), lambda i, j: (i, _zr_block(j)))]
        + [w_spec] * int(emit_w) + cast_specs,
        out_shape=[jax.ShapeDtypeStruct((m, D_LRU), F32), jax.ShapeDtypeStruct((m, ZR_WIDTH), BF16)]
        + [jax.ShapeDtypeStruct(w.shape, BF16)] * int(emit_w) + cast_shapes,
        scratch_shapes=[pltpu.VMEM((bm, D_MODEL), BF16)],
        compiler_params=_params(("parallel", "arbitrary")),
        name="inproj",
    )(x, g, perm, w, *cast)


def _lru_kernel(*refs, nb, bt, n_tiles, first_pos_is_zero, n_cast):
    ((x_ref, conv0_ref, h0_ref, cw_ref, cb_ref, wax_ref, bax_ref, lam_ref, proj_ref),
     cast_src, (ya_ref, conv_ref, ht_ref), cast_dst,
     (xe_scr, a_scr, b_scr, hq_scr)) = _split_refs(refs, 9, 3, n_cast)
    c = pl.program_id(0)
    rows = nb * bt
    pad = 2 * SUBLANES
    pipelined = bt > 1

    @pl.when(c == 0)
    def _():
        conv_ref[...] = conv0_ref[...]
        ht_ref[...] = h0_ref[...]
        if pipelined:
            xe_scr[:pad - 3 * nb, :] = jnp.zeros((pad - 3 * nb, D_LRU), F32)
            hq_scr[...] = jnp.zeros(hq_scr.shape, BF16)

    nlam = -lam_ref[...]
    sp = jnp.maximum(nlam, 0.0) + jnp.log1p(jnp.exp(-jnp.abs(nlam)))
    k_tanh = LRU_C * sp
    k_exp2 = (-LRU_C * LOG2E) * sp

    if bt > 1:
        for k in range(CONV_W - 1):
            xe_scr[pad - (3 - k) * nb:pad - (2 - k) * nb, :] = conv_ref[:, k, :]
        xe_scr[pad:, :] = x_ref[...]
        is_first = (lax.broadcasted_iota(jnp.int32, (rows, LRU_BW), 0) < nb) & (c == 0)

    def project_prev(q):
        qs = slice(q * LRU_BW, (q + 1) * LRU_BW)
        ya = jnp.dot(hq_scr[...], proj_ref[:, qs], preferred_element_type=F32)
        ya_ref[:, qs] = ya.astype(BF16)

    lo = _lo_rows((SUBLANES, LANES))
    h_last = {}

    def scan_columns(col_blocks):
        cols = {cb: slice(cb * LANES, (cb + 1) * LANES) for cb in col_blocks}
        hs = {cb: jnp.concatenate([ht_ref[:, cl], ht_ref[:, cl]], axis=0) for cb, cl in cols.items()}
        for k in range(rows // SUBLANES):
            rs = slice(k * SUBLANES, (k + 1) * SUBLANES)
            for cb, cl in cols.items():
                av, bv = a_scr[rs, cl], b_scr[rs, cl]
                first = av * hs[cb] + bv
                second = av * pltpu.roll(first, HALF, axis=0) + bv
                h = jnp.where(lo, first, second)
                b_scr[rs, cl] = h
                hs[cb] = pltpu.roll(h, HALF, axis=0)
        h_last.update(hs)

    for hb in range(H_LRU):
        cs = slice(hb * LRU_BW, (hb + 1) * LRU_BW)
        if pipelined:
            project_prev(hb)
        if hb < n_cast:
            cast_dst[hb][...] = cast_src[hb][...].astype(BF16)
        if bt > 1:
            xs = pltpu.roll(xe_scr[:, cs], nb, axis=0)
            xc = (cb_ref[:, cs] + xs[pad - 2 * nb:pad + rows - 2 * nb] * cw_ref[0:1, cs]
                  + xe_scr[pad - 2 * nb:pad + rows - 2 * nb, cs] * cw_ref[1:2, cs]
                  + xs[pad:] * cw_ref[2:3, cs]
                  + xe_scr[pad:, cs] * cw_ref[3:4, cs])
        else:
            xc = (cb_ref[:, cs] + conv_ref[:, 0, cs] * cw_ref[0:1, cs]
                  + conv_ref[:, 1, cs] * cw_ref[1:2, cs] + conv_ref[:, 2, cs] * cw_ref[2:3, cs]
                  + x_ref[:, cs] * cw_ref[3:4, cs])
        gates = jnp.dot(xc.astype(BF16), wax_ref[hb], preferred_element_type=F32) + bax_ref[hb]
        r = _sigmoid(gates[:, :LRU_BW])
        i = _sigmoid(gates[:, LRU_BW:])
        a = jnp.exp2(r * k_exp2[:, cs])
        m2 = jnp.tanh(r * k_tanh[:, cs]) * (1.0 + a * a)
        mult = jnp.where(m2 > 0.0, m2 * lax.rsqrt(m2), 0.0)
        if first_pos_is_zero:
            mult = jnp.where(is_first, 1.0, mult)
        a_scr[:, cs] = a
        b_scr[:, cs] = mult * (i * xc)
        if pipelined:
            scan_columns(range(hb * LRU_BW // LANES, (hb + 1) * LRU_BW // LANES))

    if not pipelined:
        conv_ref[:, 0, :] = conv_ref[:, 1, :]
        conv_ref[:, 1, :] = conv_ref[:, 2, :]
        conv_ref[:, 2, :] = x_ref[...]
        h = a_scr[...] * ht_ref[...] + b_scr[...]
        ht_ref[...] = h
        ya = jnp.dot(h.astype(BF16), proj_ref[...], preferred_element_type=F32)
        ya_ref[...] = ya.astype(BF16)
        return

    @pl.when(c < n_tiles)
    def _():
        for k in range(CONV_W - 1):
            conv_ref[:, k, :] = x_ref[rows - (3 - k) * nb:rows - (2 - k) * nb, :]
        for cb, h in h_last.items():
            ht_ref[:, cb * LANES:(cb + 1) * LANES] = h[:HALF]
        hq_scr[...] = b_scr[...].astype(BF16)


def _lru(z, conv0, h0, cw, cb, wax, bax, lam, proj, *, nb, seq_len, bt, first_pos_is_zero, cast=()):
    assert bt == 1 or (nb == HALF and bt % 2 == 0 and bt >= 4)
    assert len(cast) <= H_LRU
    rows, n_tiles = nb * bt, seq_len // bt
    pipelined = bt > 1
    st = pl.BlockSpec((nb, D_LRU), lambda c: (0, 0))
    st_sds = jax.ShapeDtypeStruct((nb, D_LRU), F32)
    cst = pl.BlockSpec((nb, CONV_W - 1, D_LRU), lambda c: (0, 0, 0))
    cst_sds = jax.ShapeDtypeStruct((nb, CONV_W - 1, D_LRU), F32)
    x_tile = pl.BlockSpec((rows, D_LRU), lambda c: (jnp.minimum(c, n_tiles - 1), 0))
    ya_tile = pl.BlockSpec((rows, D_LRU), lambda c: (jnp.maximum(c - int(pipelined), 0), 0))
    kern = functools.partial(_lru_kernel, nb=nb, bt=bt, n_tiles=n_tiles,
                             first_pos_is_zero=first_pos_is_zero, n_cast=len(cast))
    cast_specs, cast_shapes = _side_cast_specs(cast, n_tiles, lambda c: jnp.minimum(c, n_tiles - 1))
    small = (SUBLANES, LANES)
    ya, conv, ht, *cast_out = pl.pallas_call(
        kern,
        grid=(n_tiles + int(pipelined),),
        in_specs=[x_tile, cst, st,
                  _const_spec((CONV_W, D_LRU)), _const_spec((1, D_LRU)),
                  _const_spec((H_LRU, LRU_BW, 2 * LRU_BW)), _const_spec((H_LRU, 1, 2 * LRU_BW)),
                  _const_spec((1, D_LRU)), _const_spec((D_LRU, D_MODEL))] + cast_specs,
        out_specs=[ya_tile, cst, st] + cast_specs,
        out_shape=[jax.ShapeDtypeStruct((nb * seq_len, D_LRU), BF16), cst_sds, st_sds] + cast_shapes,
        scratch_shapes=[pltpu.VMEM((rows + 2 * SUBLANES, D_LRU) if pipelined else small, F32),
                        pltpu.VMEM((rows, D_LRU), F32), pltpu.VMEM((rows, D_LRU), F32),
                        pltpu.VMEM((rows, D_LRU) if pipelined else (2 * SUBLANES, LANES), BF16)],
        compiler_params=_params(("arbitrary",)),
        name="lru",
    )(z, conv0, h0, cw, cb, wax, bax, lam, proj, *cast)
    return ya, conv, ht, cast_out


def _gelu_tanh(x):
    c = math.sqrt(2.0 / math.pi)
    return 0.5 * x * (1.0 + jnp.tanh(c * (x + 0.044715 * (x * x * x))))


S5_GROUPS = 4
S5_GROUP_SLABS = S5_SLABS // S5_GROUPS
S5_GROUP_COLS = S5_N // S5_GROUPS
GLU_GROUP_COLS = D_MODEL // S5_GROUPS


def _s5_b_proj(ub, up, wb_ref, xr_scr, xi_scr, j):
    js = slice(j * S5_SLAB_IN, (j + 1) * S5_SLAB_IN)
    if up is None:
        w = jnp.dot(ub[:, js], wb_ref[j, :S5_SLAB_IN, :], preferred_element_type=F32)
    else:
        w = jnp.dot(jnp.concatenate([ub[:, js], up[:, js]], axis=1), wb_ref[j],
                    preferred_element_type=F32)
    xr_scr[:, j * S5_SLAB_ST:(j + 1) * S5_SLAB_ST] = w[:, :S5_SLAB_ST]
    xi_scr[:, j * S5_SLAB_ST:(j + 1) * S5_SLAB_ST] = w[:, S5_SLAB_ST:]


def _s5_c_proj(xr_scr, xi_scr, wc_ref, y_scr, j):
    ss = slice(j * S5_SLAB_ST, (j + 1) * S5_SLAB_ST)
    xs = jnp.concatenate([xr_scr[:, ss].astype(BF16), xi_scr[:, ss].astype(BF16)], axis=1)
    y_scr[:, j * S5_SLAB_IN:(j + 1) * S5_SLAB_IN] = jnp.dot(xs, wc_ref[j], preferred_element_type=F32)


def _s5_kernel(*refs, nb, bt, n_tiles, n_cast):
    ((u_ref, s0r_ref, s0i_ref, car_ref, cai_ref, wb_ref, wc_ref, d_ref, wv_ref, wg_ref), cast_src,
     (yb_ref, sr_ref, si_ref), cast_dst,
     (xr_scr, xi_scr, pr_scr, pi_scr, y_scr, v_scr, ulast_scr)) = _split_refs(refs, 10, 3, n_cast)
    c = pl.program_id(0)
    rows = nb * bt
    ub = u_ref[...]
    u = ub.astype(F32)

    if bt == 1:
        for j in range(S5_SLABS):
            _s5_b_proj(ub, None, wb_ref, xr_scr, xi_scr, j)
        ar, ai = car_ref[0:1, :], cai_ref[0:1, :]
        pr, pi = s0r_ref[...], s0i_ref[...]
        nr = ar * pr - ai * pi + xr_scr[...]
        ni = ar * pi + ai * pr + xi_scr[...]
        xr_scr[...] = nr
        xi_scr[...] = ni
        sr_ref[...] = nr
        si_ref[...] = ni
        for j in range(S5_SLABS):
            _s5_c_proj(xr_scr, xi_scr, wc_ref, y_scr, j)
        v = _gelu_tanh(y_scr[...] + d_ref[...] * u).astype(BF16)
        yv = jnp.dot(v, wv_ref[...], preferred_element_type=F32)
        yg = jnp.dot(v, wg_ref[...], preferred_element_type=F32)
        yb_ref[...] = (yv * _sigmoid(yg)).astype(BF16)
        return

    @pl.when(c == 0)
    def _():
        pr_scr[...] = jnp.zeros(pr_scr.shape, F32)
        pi_scr[...] = jnp.zeros(pi_scr.shape, F32)
        ulast_scr[...] = jnp.zeros(ulast_scr.shape, F32)
        v_scr[...] = jnp.zeros(v_scr.shape, BF16)

    up = pltpu.roll(jnp.concatenate([ulast_scr[...], u], axis=0), HALF, axis=0)[SUBLANES:]
    up = up.astype(BF16)
    ulast_scr[...] = u[rows - SUBLANES:, :]
    lo = _lo_rows((SUBLANES, LANES))
    first = c == 0
    v_prev = v_scr[...]
    for grp in range(S5_GROUPS):
        slabs = range(grp * S5_GROUP_SLABS, (grp + 1) * S5_GROUP_SLABS)
        for j in slabs:
            _s5_b_proj(ub, up, wb_ref, xr_scr, xi_scr, j)
        for j in range((grp - 1) * S5_GROUP_SLABS, grp * S5_GROUP_SLABS) if grp else ():
            _s5_c_proj(xr_scr, xi_scr, wc_ref, y_scr, j)
        gs = slice(grp * GLU_GROUP_COLS, (grp + 1) * GLU_GROUP_COLS)
        yv = jnp.dot(v_prev, wv_ref[:, gs], preferred_element_type=F32)
        yg = jnp.dot(v_prev, wg_ref[:, gs], preferred_element_type=F32)
        yb_ref[:, gs] = (yv * _sigmoid(yg)).astype(BF16)
        if grp < n_cast:
            cast_dst[grp][...] = cast_src[grp][...].astype(BF16)
        cols = [slice(grp * S5_GROUP_COLS + cb * LANES, grp * S5_GROUP_COLS + (cb + 1) * LANES)
                for cb in range(S5_GROUP_COLS // LANES)]
        carry, coeff = [], []
        for cl in cols:
            ar, ai = car_ref[:, cl], cai_ref[:, cl]
            s0r = jnp.concatenate([s0r_ref[:, cl], s0r_ref[:, cl]], axis=0)
            s0i = jnp.concatenate([s0i_ref[:, cl], s0i_ref[:, cl]], axis=0)
            xr_scr[:SUBLANES, cl] += jnp.where(first, ar * s0r - ai * s0i, 0.0)
            xi_scr[:SUBLANES, cl] += jnp.where(first, ar * s0i + ai * s0r, 0.0)
            coeff.append((jnp.where(lo, pltpu.roll(ar, HALF, axis=0), ar),
                          jnp.where(lo, pltpu.roll(ai, HALF, axis=0), ai)))
            carry.append((pr_scr[:, cl], pi_scr[:, cl]))
        for k in range(rows // SUBLANES):
            rs = slice(k * SUBLANES, (k + 1) * SUBLANES)
            for cb, cl in enumerate(cols):
                (pr, pi), (ar, ai) = carry[cb], coeff[cb]
                nr = ar * pr - ai * pi + xr_scr[rs, cl]
                ni = ar * pi + ai * pr + xi_scr[rs, cl]
                xr_scr[rs, cl] = nr
                xi_scr[rs, cl] = ni
                carry[cb] = (nr, ni)
        for cb, cl in enumerate(cols):
            pr_scr[:, cl], pi_scr[:, cl] = carry[cb]
    for j in range((S5_GROUPS - 1) * S5_GROUP_SLABS, S5_SLABS):
        _s5_c_proj(xr_scr, xi_scr, wc_ref, y_scr, j)
    v_scr[...] = _gelu_tanh(y_scr[...] + d_ref[...] * u).astype(BF16)

    @pl.when(c == n_tiles - 1)
    def _():
        sr_ref[...] = pr_scr[HALF:, :]
        si_ref[...] = pi_scr[HALF:, :]


def _s5(z, s0r, s0i, ca_re, ca_im, wb, wc, d, wv, wg, *, nb, seq_len, bt, cast=()):
    assert bt == 1 or (nb == HALF and bt % 2 == 0)
    assert len(cast) <= S5_GROUPS
    rows, n_tiles = nb * bt, seq_len // bt
    pipelined = bt > 1
    st = pl.BlockSpec((nb, S5_N), lambda c: (0, 0))
    st_sds = jax.ShapeDtypeStruct((nb, S5_N), F32)
    x_s5_block = (ZR_WIDTH - D_S5) // D_S5
    cast_specs, cast_shapes = _side_cast_specs(cast, n_tiles, lambda c: jnp.minimum(c, n_tiles - 1))
    yb, sr, si, *cast_out = pl.pallas_call(
        functools.partial(_s5_kernel, nb=nb, bt=bt, n_tiles=n_tiles, n_cast=len(cast)),
        grid=(n_tiles + int(pipelined),),
        in_specs=[pl.BlockSpec((rows, D_S5), lambda c: (jnp.minimum(c, n_tiles - 1), x_s5_block)),
                  st, st, _const_spec((SUBLANES, S5_N)), _const_spec((SUBLANES, S5_N)),
                  _const_spec((S5_SLABS, 2 * S5_SLAB_IN, 2 * S5_SLAB_ST)),
                  _const_spec((S5_SLABS, 2 * S5_SLAB_ST, S5_SLAB_IN)),
                  _const_spec((1, D_S5)), _const_spec((D_S5, D_MODEL)),
                  _const_spec((D_S5, D_MODEL))] + cast_specs,
        out_specs=[pl.BlockSpec((rows, D_MODEL), lambda c: (jnp.maximum(c - int(pipelined), 0), 0)),
                   st, st] + cast_specs,
        out_shape=[jax.ShapeDtypeStruct((nb * seq_len, D_MODEL), BF16), st_sds, st_sds] + cast_shapes,
        scratch_shapes=[pltpu.VMEM((rows, S5_N), F32), pltpu.VMEM((rows, S5_N), F32),
                        pltpu.VMEM((SUBLANES, S5_N), F32), pltpu.VMEM((SUBLANES, S5_N), F32),
                        pltpu.VMEM((rows, D_S5), F32), pltpu.VMEM((rows, D_S5), BF16),
                        pltpu.VMEM((SUBLANES, D_S5), F32)],
        compiler_params=_params(("arbitrary",)),
        name="s5",
    )(z, s0r, s0i, ca_re, ca_im, wb, wc, d, wv, wg, *cast)
    return yb, sr, si, cast_out


def _merge_kernel(ya_ref, yb_ref, ga_ref, gb_ref, x_ref, perm_ref, w_ref, g_ref, x1_ref, u2_ref,
                  *, nb, bt):
    def gated(rs):
        f32 = lambda ref: ref[rs, :].astype(F32)
        merged = _sigmoid(f32(ga_ref)) * f32(ya_ref) + _sigmoid(f32(gb_ref)) * f32(yb_ref)
        return merged.astype(BF16)

    if bt == 1:
        x1 = x_ref[...] + jnp.dot(gated(slice(None)), w_ref[...], preferred_element_type=F32)
        x1_ref[...] = x1
        u2_ref[...] = _rmsnorm(x1, g_ref[...]).astype(BF16)
        return
    n_sub = bt // PERM_T
    sub_rows = lambda s: slice(s * nb * PERM_T, (s + 1) * nb * PERM_T)

    def finish(s, delta):
        ts = slice(s * PERM_T, (s + 1) * PERM_T)
        for b in range(nb):
            x1 = x_ref[b, ts, :] + delta[b * PERM_T:(b + 1) * PERM_T]
            x1_ref[b, ts, :] = x1
            u2_ref[b, ts, :] = _rmsnorm(x1, g_ref[...]).astype(BF16)

    merged = gated(sub_rows(0))
    pending = None
    for s in range(n_sub):
        mb = jnp.dot(perm_ref[...], merged, preferred_element_type=F32).astype(BF16)
        delta = jnp.dot(mb, w_ref[...], preferred_element_type=F32)
        if s + 1 < n_sub:
            merged = gated(sub_rows(s + 1))
        if pending is not None:
            finish(*pending)
        pending = (s, delta)
    finish(*pending)


def _merge(ya, yb, zr, x, w_out, g, *, nb, seq_len, bt):
    assert bt == 1 or bt % PERM_T == 0
    bm = nb * bt
    row = lambda col: pl.BlockSpec((bm, D_MODEL), lambda i: (i, col))
    perm = _perm_matrix(nb, False) if bt > 1 else jnp.zeros((SUBLANES, LANES), BF16)
    tile = _row_tile_spec(nb, bt, D_MODEL)
    return pl.pallas_call(
        functools.partial(_merge_kernel, nb=nb, bt=bt),
        grid=(seq_len // bt,),
        in_specs=[row(0), row(0), row(0), row(1), tile, _const_spec(perm.shape),
                  _const_spec((D_MODEL, D_MODEL)), _const_spec((1, D_MODEL))],
        out_specs=(tile, tile),
        out_shape=(jax.ShapeDtypeStruct(x.shape, F32), jax.ShapeDtypeStruct(x.shape, BF16)),
        compiler_params=_params(("parallel",)),
        name="merge",
    )(ya, yb, zr, zr, x, perm, w_out, g)


def _ffn_kernel(u_ref, x1_ref, ue_ref, x1e_ref, wg_ref, wu_ref, wd_ref, g_ref, y_ref, ye_ref,
                acc_scr, acce_scr):
    i, f = pl.program_id(0), pl.program_id(1)
    first, last = f == 0, f == pl.num_programs(1) - 1
    with_extra = i == 0
    bm = u_ref.shape[0]

    def down_proj(u):
        hg = jnp.dot(u, wg_ref[...], preferred_element_type=F32)
        hu = jnp.dot(u, wu_ref[...], preferred_element_type=F32)
        hid = (hg * _sigmoid(hg) * hu).astype(BF16)
        return jnp.dot(hid, wd_ref[...], preferred_element_type=F32)

    @pl.when(first)
    def _():
        acc_scr[...] = x1_ref[...]

    @pl.when(first & with_extra)
    def _():
        acce_scr[...] = x1e_ref[...]

    @pl.when(with_extra)
    def _():
        down = down_proj(jnp.concatenate([u_ref[...], ue_ref[...]], axis=0))
        acc_scr[...] += down[:bm]
        acce_scr[...] += down[bm:]

    @pl.when(jnp.logical_not(with_extra))
    def _():
        acc_scr[...] += down_proj(u_ref[...])

    @pl.when(last)
    def _():
        y_ref[...] = _rmsnorm(acc_scr[...], g_ref[...])

    @pl.when(last & with_extra)
    def _():
        ye_ref[...] = _rmsnorm(acce_scr[...], g_ref[...])


def _ffn(u2, x1, u2_extra, x1_extra, wg, wu, wd, g, bm, bf):
    m, me = x1.shape[0], x1_extra.shape[0]
    row = pl.BlockSpec((bm, D_MODEL), lambda i, f: (i, 0))
    extra = pl.BlockSpec((me, D_MODEL), lambda i, f: (0, 0))
    return pl.pallas_call(
        _ffn_kernel,
        grid=(m // bm, D_FF // bf),
        in_specs=[row, row, extra, extra,
                  pl.BlockSpec((D_MODEL, bf), lambda i, f: (0, f)),
                  pl.BlockSpec((D_MODEL, bf), lambda i, f: (0, f)),
                  pl.BlockSpec((bf, D_MODEL), lambda i, f: (f, 0)),
                  pl.BlockSpec((1, D_MODEL), lambda i, f: (0, 0))],
        out_specs=(row, extra),
        out_shape=(jax.ShapeDtypeStruct((m, D_MODEL), F32), jax.ShapeDtypeStruct((me, D_MODEL), F32)),
        scratch_shapes=[pltpu.VMEM((bm, D_MODEL), F32), pltpu.VMEM((me, D_MODEL), F32)],
        compiler_params=_params(("arbitrary", "arbitrary")),
        name="ffn",
    )(u2, x1, u2_extra, x1_extra, wg, wu, wd, g)


PROMPT_TILES = dict(bt_in=256, bt_lru=128, bt_s5=64, bt_merge=128)
SINGLE_STEP_TILES = dict(bt_in=1, bt_lru=1, bt_s5=1, bt_merge=1)
FFN_BM = 512
FFN_BF = 512


def _mixer(x, conv0, h0, s0r, s0i, w, w32, *, bt_in, bt_lru, bt_s5, bt_merge, first_pos_is_zero,
           z=None):
    nb, seq_len = (x.shape[0], 1) if x.ndim == 2 else x.shape[:2]
    seq = dict(nb=nb, seq_len=seq_len)
    w = dict(w)
    take = lambda *names: tuple(w32[n] for n in names) if w32 else ()
    keep = lambda names, arrs: w.update(zip(names, arrs)) if w32 else None

    if z is None:
        zl, zr, *cast = _inproj(x, w['norm_mix_g'], w['w_in'], bt=bt_in, cast=take('proj'), **seq)
        keep(('proj',), cast)
    else:
        zl, zr = z
    ya, conv_n, h_n, cast = _lru(zl, conv0, h0, w['cw'], w['cb'], w['wax'], w['bax'], w['lam'],
                                 w['proj'], bt=bt_lru, first_pos_is_zero=first_pos_is_zero,
                                 cast=take('w_out', 'wv', 'wg'), **seq)
    keep(('w_out', 'wv', 'wg'), cast)
    yb, sr, si, cast = _s5(zr, s0r, s0i, w['ca_re'], w['ca_im'], w['wb'], w['wc'], w['d'], w['wv'],
                           w['wg'], bt=bt_s5, cast=take('ffn_wg', 'ffn_wu', 'ffn_wd'), **seq)
    keep(('ffn_wg', 'ffn_wu', 'ffn_wd'), cast)
    x1, u2 = _merge(ya, yb, zr, x, w['w_out'], w['norm_ffn_g'], bt=bt_merge, **seq)
    m = nb * seq_len
    return (x1.reshape(m, D_MODEL), u2.reshape(m, D_MODEL), conv_n, h_n, sr, si), w


def kernel(x_prompt, x_sample, state_lru_conv, state_lru_h, state_s5_re, state_s5_im, norm_mix_g, w_in, lru_conv_w, lru_conv_b, lru_wa, lru_ba, lru_wx, lru_bx, lru_lambda, lru_proj, s5_lambda_re, s5_lambda_im, s5_log_dt, s5_b_re, s5_b_im, s5_c_re, s5_c_im, s5_d, s5_glu_wv, s5_glu_wg, w_out, norm_ffn_g, ffn_w_gate, ffn_w_up, ffn_w_down, norm_final_g):
    batch, seq, _ = x_prompt.shape
    dec_batch = x_sample.shape[0]
    l = 0
    ca_re, ca_im, wb, wc = _s5_prepare(s5_lambda_re[l], s5_lambda_im[l], s5_log_dt[l],
                                       s5_b_re[l], s5_b_im[l], s5_c_re[l], s5_c_im[l])
    w = dict(
        norm_mix_g=norm_mix_g[l][None, :],
        cw=lru_conv_w[l], cb=lru_conv_b[l][None, :],
        wax=jnp.concatenate([lru_wa[l], lru_wx[l]], axis=-1).astype(BF16),
        bax=jnp.concatenate([lru_ba[l], lru_bx[l]], axis=-1)[:, None, :],
        lam=lru_lambda[l][None, :],
        ca_re=ca_re, ca_im=ca_im, wb=wb, wc=wc, d=s5_d[l].reshape(1, D_S5),
        norm_ffn_g=norm_ffn_g[l][None, :], norm_final_g=norm_final_g[None, :],
    )
    w32 = dict(proj=lru_proj[l], w_out=w_out[l], wv=s5_glu_wv[l], wg=s5_glu_wg[l],
               ffn_wg=ffn_w_gate[l], ffn_wu=ffn_w_up[l], ffn_wd=ffn_w_down[l])
    zeros = lambda *s: jnp.zeros(s, F32)
    xs = x_sample.reshape(dec_batch, D_MODEL)
    zl_s, zr_s, w['w_in'] = _inproj(xs, w['norm_mix_g'], w_in[l], nb=dec_batch, seq_len=1, bt=1)
    (x1_p, u2_p, conv_p, h_p, sr_p, si_p), w = _mixer(
        x_prompt, zeros(batch, CONV_W - 1, D_LRU), zeros(batch, D_LRU),
        zeros(batch, S5_N), zeros(batch, S5_N), w, w32,
        first_pos_is_zero=True, **PROMPT_TILES)
    (x1_s, u2_s, conv_s, h_s, sr_s, si_s), _ = _mixer(
        xs, state_lru_conv[l], state_lru_h[l],
        state_s5_re[l].reshape(dec_batch, S5_N), state_s5_im[l].reshape(dec_batch, S5_N), w, None,
        first_pos_is_zero=False, z=(zl_s, zr_s), **SINGLE_STEP_TILES)
    yp, ys = _ffn(u2_p, x1_p, u2_s, x1_s, w['ffn_wg'], w['ffn_wu'], w['ffn_wd'], w['norm_final_g'],
                  bm=FFN_BM, bf=FFN_BF)
    st = lambda a, n: a.reshape(1, n, S5_G, S5_P)
    return (yp.reshape(batch, seq, D_MODEL), ys.reshape(dec_batch, 1, D_MODEL),
            conv_p[None], h_p[None], st(sr_p, batch), st(si_p, batch),
            conv_s[None], h_s[None], st(sr_s, dec_batch), st(si_s, dec_batch))
```
